```python
import jax
import jax.numpy as jnp
from jax import lax
import numpy as np

D_MODEL = 2048
BATCH = 8
SEQ = 4096
DEPTH = 1
DEC_BATCH = 16
DEC_SEQ = 16
PAST_LEN = 1024

CHUNK = 64
D_CONV = 1024
CONV_WIDTH = 31
N_HEADS = 16
N_KV_HEADS = 2
HEAD_DIM = 64
GROUP = N_HEADS // N_KV_HEADS
ATTN_W = N_HEADS * HEAD_DIM
KV_W = N_KV_HEADS * HEAD_DIM
WINDOW = 128
WINDOW_CHUNKS = WINDOW // CHUNK
N_BRANCH = 2
EPS = 1e-6
NEG = -1e30
SCALE = HEAD_DIM ** -0.5
IN_COLS = 3 * D_CONV + 2 * ATTN_W + 2 * KV_W + N_BRANCH * D_MODEL
SPLIT_POINTS = (D_CONV, 2 * D_CONV, 3 * D_CONV, 3 * D_CONV + ATTN_W, 3 * D_CONV + ATTN_W + KV_W, 3 * D_CONV + ATTN_W + 2 * KV_W, 3 * D_CONV + 2 * ATTN_W + 2 * KV_W, 3 * D_CONV + 2 * ATTN_W + 2 * KV_W + D_MODEL)

kernel_name = 'hybrid_stream_conv_swa_step'


def _rmsnorm(x, g):
    xf = x.astype(jnp.float32)
    y = xf * lax.rsqrt(jnp.mean(xf * xf, axis=-1, keepdims=True) + EPS) * g.astype(jnp.float32)
    return y.astype(x.dtype)


def _layernorm(x, g, b):
    xf = x.astype(jnp.float32)
    mu = jnp.mean(xf, axis=-1, keepdims=True)
    xc = xf - mu
    y = xc * lax.rsqrt(jnp.mean(xc * xc, axis=-1, keepdims=True) + EPS) * g.astype(jnp.float32) + b.astype(jnp.float32)
    return y.astype(x.dtype)


def _alibi_slopes():
    h = jnp.arange(1, N_HEADS + 1, dtype=jnp.float32)
    return jnp.exp2(-8.0 * h / N_HEADS).reshape(N_KV_HEADS, GROUP)


def _sink_softmax(scores, sink):
    s = jnp.broadcast_to(sink.astype(jnp.float32).reshape(N_KV_HEADS, GROUP, 1, 1), scores.shape[:-1] + (1,))
    p = jax.nn.softmax(jnp.concatenate([scores, s], axis=-1), axis=-1)
    return p[..., :-1]


def _in_project(x, norm_g, w_in):
    h = _rmsnorm(x, norm_g)
    return jnp.split(h @ w_in, SPLIT_POINTS, axis=-1)


def _conv_branch(a, b, gate, left, conv_w, conv_b, ln_g, ln_b, w_pw):
    glu = a * jax.nn.sigmoid(b)
    full = jnp.concatenate([left.astype(glu.dtype), glu], axis=1)
    y = lax.conv_general_dilated(full, conv_w[:, None, :].astype(full.dtype), window_strides=(1,), padding='VALID', dimension_numbers=('NWC', 'WIO', 'NWC'), feature_group_count=D_CONV) + conv_b
    z = jax.nn.silu(_layernorm(y, ln_g, ln_b)) * jax.nn.silu(gate)
    return z @ w_pw, full[:, -(CONV_WIDTH - 1):]


def _attn_prompt(q, k, v, sink):
    B, S, _ = q.shape
    NC = S // CHUNK
    NK = (WINDOW_CHUNKS + 1) * CHUNK
    qb = q.reshape(B, NC, CHUNK, N_KV_HEADS, GROUP, HEAD_DIM)
    kc = k.reshape(B, NC, CHUNK, N_KV_HEADS, HEAD_DIM)
    vc = v.reshape(B, NC, CHUNK, N_KV_HEADS, HEAD_DIM)
    pad = ((0, 0), (WINDOW_CHUNKS, 0), (0, 0), (0, 0), (0, 0))
    kp = jnp.pad(kc, pad)
    vp = jnp.pad(vc, pad)
    kb = jnp.concatenate([kp[:, j:j + NC] for j in range(WINDOW_CHUNKS + 1)], axis=2)
    vb = jnp.concatenate([vp[:, j:j + NC] for j in range(WINDOW_CHUNKS + 1)], axis=2)
    scores = jnp.einsum('bcqkgd,bcskd->bckgqs', qb, kb, preferred_element_type=jnp.float32) * SCALE
    qi = jnp.arange(CHUNK)
    sj = jnp.arange(NK)
    dist = jnp.abs(qi[:, None] + WINDOW - sj[None, :]).astype(jnp.float32)
    bias = -_alibi_slopes()[:, :, None, None] * dist
    valid = (jnp.arange(NC)[:, None] * CHUNK - WINDOW + sj[None, :]) >= 0
    scores = jnp.where(valid[None, :, None, None, None, :], scores + bias, NEG)
    probs = _sink_softmax(scores, sink)
    out = jnp.einsum('bckgqs,bcskd->bcqkgd', probs.astype(vb.dtype), vb)
    return out.reshape(B, S, ATTN_W)


def _attn_sample(q, k, v, k_cache, v_cache, sink):
    N, T, _ = q.shape
    L = k_cache.shape[1]
    k_all = jnp.concatenate([k_cache.astype(k.dtype), k.reshape(N, T, N_KV_HEADS, HEAD_DIM)], axis=1)
    v_all = jnp.concatenate([v_cache.astype(v.dtype), v.reshape(N, T, N_KV_HEADS, HEAD_DIM)], axis=1)
    qh = q.reshape(N, T, N_KV_HEADS, GROUP, HEAD_DIM)
    scores = jnp.einsum('btkgd,bskd->bkgts', qh, k_all, preferred_element_type=jnp.float32) * SCALE
    tpos = PAST_LEN + jnp.arange(T)
    spos = PAST_LEN - L + jnp.arange(L + T)
    tc = tpos[:, None] // CHUNK
    sc = spos[None, :] // CHUNK
    valid = (sc >= tc - WINDOW_CHUNKS) & (sc <= tc) & (spos[None, :] >= 0)
    dist = jnp.abs(tpos[:, None] - spos[None, :]).astype(jnp.float32)
    bias = -_alibi_slopes()[:, :, None, None] * dist
    scores = jnp.where(valid[None, None, None], scores + bias, NEG)
    probs = _sink_softmax(scores, sink)
    out = jnp.einsum('bkgts,bskd->btkgd', probs.astype(v_all.dtype), v_all)
    return out.reshape(N, T, ATTN_W), k_all[:, -L:], v_all[:, -L:]


def _merge(conv_o, attn_o, m_conv, m_attn, w_out):
    return (jax.nn.sigmoid(m_conv) * conv_o + jax.nn.sigmoid(m_attn) * attn_o) @ w_out


def setup_inputs(seed: int = 0) -> dict:
    key = jax.random.key(seed)
    ks = jax.random.split(key, 17)
    f32 = jnp.float32
    nrm = lambda k, shape: jax.random.normal(k, shape, dtype=f32)
    return {
        'x_prompt': nrm(ks[0], (BATCH, SEQ, D_MODEL)),
        'x_sample': nrm(ks[1], (DEC_BATCH, DEC_SEQ, D_MODEL)),
        'cache_k': nrm(ks[2], (DEPTH, DEC_BATCH, WINDOW, N_KV_HEADS, HEAD_DIM)),
        'cache_v': nrm(ks[3], (DEPTH, DEC_BATCH, WINDOW, N_KV_HEADS, HEAD_DIM)),
        'state_conv': 0.5 * nrm(ks[4], (DEPTH, DEC_BATCH, CONV_WIDTH - 1, D_CONV)),
        'norm_g': 1.0 + 0.02 * nrm(ks[5], (DEPTH, D_MODEL)),
        'w_in': nrm(ks[6], (DEPTH, D_MODEL, IN_COLS)) * D_MODEL ** -0.5,
        'conv_w': nrm(ks[7], (DEPTH, CONV_WIDTH, D_CONV)) * CONV_WIDTH ** -0.5,
        'conv_b': 0.02 * nrm(ks[8], (DEPTH, D_CONV)),
        'ln_g': 1.0 + 0.02 * nrm(ks[9], (DEPTH, D_CONV)),
        'ln_b': 0.02 * nrm(ks[10], (DEPTH, D_CONV)),
        'w_conv_pw': nrm(ks[11], (DEPTH, D_CONV, D_MODEL)) * D_CONV ** -0.5,
        'attn_sink': 0.5 * nrm(ks[12], (DEPTH, N_HEADS)),
        'w_o_attn': nrm(ks[13], (DEPTH, ATTN_W, D_MODEL)) * ATTN_W ** -0.5,
        'w_out': nrm(ks[14], (DEPTH, D_MODEL, D_MODEL)) * D_MODEL ** -0.5,
        'final_g': 1.0 + 0.02 * nrm(ks[15], (D_MODEL,)),
    }


def reference(x_prompt, x_sample, cache_k, cache_v, state_conv, norm_g, w_in, conv_w, conv_b, ln_g, ln_b, w_conv_pw, attn_sink, w_o_attn, w_out, final_g):
    xp = x_prompt
    xs = x_sample
    Bp, S, _ = xp.shape
    kp_l, vp_l, cp_l, ks_l, vs_l, cs_l = [], [], [], [], [], []
    for l in range(DEPTH):
        a, b, gc, q, k, v, ga, mc, ma = _in_project(xp, norm_g[l], w_in[l])
        left = jnp.zeros((Bp, CONV_WIDTH - 1, D_CONV), a.dtype)
        conv_o, c_new = _conv_branch(a, b, gc, left, conv_w[l], conv_b[l], ln_g[l], ln_b[l], w_conv_pw[l])
        attn = _attn_prompt(q, k, v, attn_sink[l])
        attn_o = (attn * jax.nn.silu(ga)) @ w_o_attn[l]
        xp = xp + _merge(conv_o, attn_o, mc, ma, w_out[l])
        kp_l.append(k.reshape(Bp, S, N_KV_HEADS, HEAD_DIM)[:, -WINDOW:])
        vp_l.append(v.reshape(Bp, S, N_KV_HEADS, HEAD_DIM)[:, -WINDOW:])
        cp_l.append(c_new)
        a, b, gc, q, k, v, ga, mc, ma = _in_project(xs, norm_g[l], w_in[l])
        conv_o, c_new = _conv_branch(a, b, gc, state_conv[l], conv_w[l], conv_b[l], ln_g[l], ln_b[l], w_conv_pw[l])
        attn, k_new, v_new = _attn_sample(q, k, v, cache_k[l], cache_v[l], attn_sink[l])
        attn_o = (attn * jax.nn.silu(ga)) @ w_o_attn[l]
        xs = xs + _merge(conv_o, attn_o, mc, ma, w_out[l])
        ks_l.append(k_new)
        vs_l.append(v_new)
        cs_l.append(c_new)
    y_prompt = _rmsnorm(xp, final_g)
    y_sample = _rmsnorm(xs, final_g)
    return (y_prompt, y_sample, jnp.stack(kp_l), jnp.stack(vp_l), jnp.stack(cp_l), jnp.stack(ks_l), jnp.stack(vs_l), jnp.stack(cs_l))
```

```python
import functools

import jax
import jax.numpy as jnp
from jax import lax
from jax.experimental import pallas as pl
from jax.experimental.pallas import tpu as pltpu

D_MODEL = 2048
D_CONV = 1024
CONV_WIDTH = 31
CONV_HIST = CONV_WIDTH - 1
N_HEADS = 16
N_KV_HEADS = 2
HEAD_DIM = 64
GROUP = N_HEADS // N_KV_HEADS
ATTN_W = N_HEADS * HEAD_DIM
KV_W = N_KV_HEADS * HEAD_DIM
CHUNK = 64
WINDOW = 128
EPS = 1e-6
NEG = -1e30
SCALE = HEAD_DIM ** -0.5

C_A, C_B, C_GC = 0, D_CONV, 2 * D_CONV
C_Q = 3 * D_CONV
C_KV = C_Q + ATTN_W
C_GA = C_KV + 2 * KV_W
FRONT_COLS = C_GA + ATTN_W
MERGE_COLS = 2 * D_MODEL

LANES = 128
HIST_PAD = 32
HIST_OFF = HIST_PAD - CONV_HIST
CONV_ROWS = 64
PROMPT_TILE = 256
VMEM_LIMIT = 56 * 1024 * 1024

_SLOPES = [2.0 ** (-8.0 * (h + 1) / N_HEADS) for h in range(N_HEADS)]

_F32 = jnp.float32
_BF16 = jnp.bfloat16


def _sigmoid(x):
    return 1.0 / (1.0 + jnp.exp(-x))


def _silu(x):
    return x * _sigmoid(x)


def _rms_rows(x, g):
    return x * lax.rsqrt(jnp.mean(x * x, axis=-1, keepdims=True) + EPS) * g


def _dot(a, b):
    return jnp.dot(a, b, preferred_element_type=_F32)


def _conv_rows(s_ref, row0, n_rows, cw_ref, cb_ref, y_ref, out_row0):
    rb = min(CONV_ROWS, n_rows)
    for c in range(D_CONV // LANES):
        cs = slice(c * LANES, (c + 1) * LANES)
        for r in range(n_rows // rb):
            base = row0 + HIST_OFF + r * rb
            acc = jnp.broadcast_to(cb_ref[:, cs], (rb, LANES))
            for j in range(CONV_WIDTH):
                acc = acc + s_ref[base + j:base + j + rb, cs] * cw_ref[j:j + 1, cs]
            y_ref[out_row0 + r * rb:out_row0 + (r + 1) * rb, cs] = acc


def _conv_gate(y, gc, lg, lb):
    mu = jnp.mean(y, axis=-1, keepdims=True)
    yc = y - mu
    ln = yc * lax.rsqrt(jnp.mean(yc * yc, axis=-1, keepdims=True) + EPS) * lg + lb
    return _silu(ln) * _silu(gc)


def _attend(q, kd, vd, sink_ref, key_pos0):
    tq = q.shape[0]
    nk = kd[0].shape[0]
    lane = lax.broadcasted_iota(jnp.int32, (tq, LANES), 1)
    low_half = lane < HEAD_DIM
    qi = lax.broadcasted_iota(jnp.int32, (tq, nk), 0)
    sj = lax.broadcasted_iota(jnp.int32, (tq, nk), 1)
    dist = jnp.abs(qi + WINDOW - sj).astype(_F32)
    valid = None if key_pos0 is None else (sj + key_pos0) >= 0
    zero = jnp.zeros((), _BF16)
    outs = []
    for kv in range(N_KV_HEADS):
        blocks = []
        for g in range(GROUP):
            h = kv * GROUP + g
            qb = q[:, (h // 2) * LANES:(h // 2 + 1) * LANES]
            blocks.append(jnp.where(low_half if h % 2 == 0 else jnp.logical_not(low_half), qb, zero))
        qs = jnp.concatenate(blocks, axis=0)
        s_all = lax.dot_general(qs, kd[kv], (((1,), (1,)), ((), ())), preferred_element_type=_F32)
        probs, inv = [], []
        for g in range(GROUP):
            h = kv * GROUP + g
            s = s_all[g * tq:(g + 1) * tq] * SCALE - _SLOPES[h] * dist
            if valid is not None:
                s = jnp.where(valid, s, NEG)
            sink = sink_ref[h]
            m = jnp.maximum(jnp.max(s, axis=-1, keepdims=True), sink)
            e = jnp.exp(s - m)
            denom = jnp.sum(e, axis=-1, keepdims=True) + jnp.exp(sink - m)
            probs.append(e.astype(_BF16))
            inv.append(1.0 / denom)
        p_all = jnp.concatenate(probs, axis=0)
        o_all = _dot(p_all, vd[kv]) * jnp.concatenate(inv, axis=0)
        for pair in range(GROUP // 2):
            outs.append(jnp.where(low_half, o_all[(2 * pair) * tq:(2 * pair + 1) * tq],
                                  o_all[(2 * pair + 1) * tq:(2 * pair + 2) * tq]))
    return jnp.concatenate(outs, axis=1)


def _dup_heads(kv_rows):
    kvb = kv_rows.astype(_BF16)
    h0, h1 = kvb[:, :HEAD_DIM], kvb[:, HEAD_DIM:]
    return jnp.concatenate([h0, h0], axis=1), jnp.concatenate([h1, h1], axis=1)


def _front_prompt_kernel(sink_ref, x_ref, g_ref, w_ref, cw_ref, cb_ref, lg_ref, lb_ref,
                         z_ref, u_ref, kwin_ref, vwin_ref, cst_ref,
                         s_ref, y_ref, k0_ref, k1_ref, v0_ref, v1_ref, q_ref, attn_ref):
    t = x_ref.shape[1]
    j = pl.program_id(1)
    last = pl.num_programs(1) - 1

    @pl.when(j == 0)
    def _():
        s_ref[0:HIST_PAD, :] = jnp.zeros((HIST_PAD, D_CONV), _F32)
        for r in (k0_ref, k1_ref, v0_ref, v1_ref):
            r[0:WINDOW, :] = jnp.zeros((WINDOW, LANES), _BF16)

    h = _rms_rows(x_ref[0], g_ref[...]).astype(_BF16)

    a = _dot(h, w_ref[:, C_A:C_A + D_CONV])
    b = _dot(h, w_ref[:, C_B:C_B + D_CONV])
    s_ref[HIST_PAD:HIST_PAD + t, :] = a * _sigmoid(b)
    _conv_rows(s_ref, 0, t, cw_ref, cb_ref, y_ref, 0)
    gc = _dot(h, w_ref[:, C_GC:C_GC + D_CONV])
    z_ref[...] = _conv_gate(y_ref[...], gc, lg_ref[...], lb_ref[...]).astype(_BF16)

    @pl.when(j == last)
    def _():
        cst_ref[0] = s_ref[HIST_PAD + t - CONV_HIST:HIST_PAD + t, :]

    s_ref[0:HIST_PAD, :] = s_ref[t:t + HIST_PAD, :]

    q_ref[...] = _dot(h, w_ref[:, C_Q:C_Q + ATTN_W]).astype(_BF16)
    kv = _dot(h, w_ref[:, C_KV:C_KV + 2 * KV_W])
    k_new, v_new = kv[:, :KV_W], kv[:, KV_W:]

    @pl.when(j == last)
    def _():
        kwin_ref[0] = k_new[t - WINDOW:, :]
        vwin_ref[0] = v_new[t - WINDOW:, :]

    kd0, kd1 = _dup_heads(k_new)
    vd0, vd1 = _dup_heads(v_new)
    k0_ref[WINDOW:WINDOW + t, :] = kd0
    k1_ref[WINDOW:WINDOW + t, :] = kd1
    v0_ref[WINDOW:WINDOW + t, :] = vd0
    v1_ref[WINDOW:WINDOW + t, :] = vd1
    nk = WINDOW + CHUNK
    for c in range(t // CHUNK):
        r0 = c * CHUNK
        kd = (k0_ref[r0:r0 + nk, :], k1_ref[r0:r0 + nk, :])
        vd = (v0_ref[r0:r0 + nk, :], v1_ref[r0:r0 + nk, :])
        key_pos0 = j * t + r0 - WINDOW
        attn_ref[r0:r0 + CHUNK, :] = _attend(q_ref[r0:r0 + CHUNK, :], kd, vd, sink_ref, key_pos0)
    for r in (k0_ref, k1_ref, v0_ref, v1_ref):
        r[0:WINDOW, :] = r[t:t + WINDOW, :]
    ga = _dot(h, w_ref[:, C_GA:C_GA + ATTN_W])
    u_ref[...] = (attn_ref[...] * _silu(ga)).astype(_BF16)


def _front_sample_kernel(sink_ref, x_ref, g_ref, w_ref, cw_ref, cb_ref, lg_ref, lb_ref,
                         ck_ref, cv_ref, sc_ref,
                         z_ref, u_ref, kwin_ref, vwin_ref, cst_ref,
                         s_ref, y_ref, attn_ref):
    n, t = ck_ref.shape[0], x_ref.shape[0] // ck_ref.shape[0]
    span = HIST_PAD + t
    h = _rms_rows(x_ref[...], g_ref[...]).astype(_BF16)

    a = _dot(h, w_ref[:, C_A:C_A + D_CONV])
    b = _dot(h, w_ref[:, C_B:C_B + D_CONV])
    glu = a * _sigmoid(b)
    for i in range(n):
        s_ref[i * span + HIST_OFF:i * span + HIST_PAD, :] = sc_ref[i]
        s_ref[i * span + HIST_PAD:(i + 1) * span, :] = glu[i * t:(i + 1) * t]
        _conv_rows(s_ref, i * span, t, cw_ref, cb_ref, y_ref, i * t)
        cst_ref[i] = s_ref[(i + 1) * span - CONV_HIST:(i + 1) * span, :]
    gc = _dot(h, w_ref[:, C_GC:C_GC + D_CONV])
    z_ref[...] = _conv_gate(y_ref[...], gc, lg_ref[...], lb_ref[...]).astype(_BF16)

    q = _dot(h, w_ref[:, C_Q:C_Q + ATTN_W]).astype(_BF16)
    kv = _dot(h, w_ref[:, C_KV:C_KV + 2 * KV_W])
    for i in range(n):
        k_all = jnp.concatenate([ck_ref[i], kv[i * t:(i + 1) * t, :KV_W]], axis=0)
        v_all = jnp.concatenate([cv_ref[i], kv[i * t:(i + 1) * t, KV_W:]], axis=0)
        kwin_ref[i] = k_all[t:, :]
        vwin_ref[i] = v_all[t:, :]
        attn_ref[i * t:(i + 1) * t, :] = _attend(q[i * t:(i + 1) * t, :], _dup_heads(k_all), _dup_heads(v_all),
                                                 sink_ref, None)
    ga = _dot(h, w_ref[:, C_GA:C_GA + ATTN_W])
    u_ref[...] = (attn_ref[...] * _silu(ga)).astype(_BF16)


def _back_kernel(x_ref, z_ref, u_ref, g_ref, wm_ref, wpw_ref, wo_ref, wout_ref, fg_ref, y_ref, m_ref):
    x = x_ref[...]
    h = _rms_rows(x, g_ref[...]).astype(_BF16)
    z = z_ref[...]
    u = u_ref[...]
    nb = 512
    for c in range(D_MODEL // nb):
        cs = slice(c * nb, (c + 1) * nb)
        mc = _dot(h, wm_ref[:, c * nb:(c + 1) * nb])
        ma = _dot(h, wm_ref[:, D_MODEL + c * nb:D_MODEL + (c + 1) * nb])
        conv_o = _dot(z, wpw_ref[:, cs])
        attn_o = _dot(u, wo_ref[:, cs])
        m_ref[:, cs] = (_sigmoid(mc) * conv_o + _sigmoid(ma) * attn_o).astype(_BF16)
    y_ref[...] = _rms_rows(x + _dot(m_ref[...], wout_ref[...]), fg_ref[...])


_VMEM = pl.BlockSpec(memory_space=pltpu.VMEM)
_SMEM = pl.BlockSpec(memory_space=pltpu.SMEM)


def _front_prompt(x, sink, norm_g, w_front, cw, cb, lg, lb):
    bsz, seq, _ = x.shape
    t = PROMPT_TILE
    nt = seq // t
    tok = lambda width, dt: jax.ShapeDtypeStruct((bsz * seq, width), dt)
    return pl.pallas_call(
        _front_prompt_kernel,
        grid=(bsz, nt),
        in_specs=[_SMEM, pl.BlockSpec((1, t, D_MODEL), lambda b, j: (b, j, 0)),
                  _VMEM, _VMEM, _VMEM, _VMEM, _VMEM, _VMEM],
        out_specs=[pl.BlockSpec((t, D_CONV), lambda b, j: (b * nt + j, 0)),
                   pl.BlockSpec((t, ATTN_W), lambda b, j: (b * nt + j, 0)),
                   pl.BlockSpec((1, WINDOW, KV_W), lambda b, j: (b, 0, 0)),
                   pl.BlockSpec((1, WINDOW, KV_W), lambda b, j: (b, 0, 0)),
                   pl.BlockSpec((1, CONV_HIST, D_CONV), lambda b, j: (b, 0, 0))],
        out_shape=[tok(D_CONV, _BF16), tok(ATTN_W, _BF16),
                   jax.ShapeDtypeStruct((bsz, WINDOW, KV_W), _F32),
                   jax.ShapeDtypeStruct((bsz, WINDOW, KV_W), _F32),
                   jax.ShapeDtypeStruct((bsz, CONV_HIST, D_CONV), _F32)],
        scratch_shapes=[pltpu.VMEM((HIST_PAD + t, D_CONV), _F32),
                        pltpu.VMEM((t, D_CONV), _F32),
                        pltpu.VMEM((WINDOW + t, LANES), _BF16),
                        pltpu.VMEM((WINDOW + t, LANES), _BF16),
                        pltpu.VMEM((WINDOW + t, LANES), _BF16),
                        pltpu.VMEM((WINDOW + t, LANES), _BF16),
                        pltpu.VMEM((t, ATTN_W), _BF16),
                        pltpu.VMEM((t, ATTN_W), _F32)],
        compiler_params=pltpu.CompilerParams(dimension_semantics=("arbitrary", "arbitrary"),
                                             vmem_limit_bytes=VMEM_LIMIT),
        name="front_prompt",
    )(sink, x, norm_g, w_front, cw, cb, lg, lb)


def _front_sample(x2d, sink, norm_g, w_front, cw, cb, lg, lb, cache_k, cache_v, state_conv):
    n = cache_k.shape[0]
    ntok = x2d.shape[0]
    t = ntok // n
    return pl.pallas_call(
        _front_sample_kernel,
        in_specs=[_SMEM] + [_VMEM] * 10,
        out_specs=[_VMEM] * 5,
        out_shape=[jax.ShapeDtypeStruct((ntok, D_CONV), _BF16),
                   jax.ShapeDtypeStruct((ntok, ATTN_W), _BF16),
                   jax.ShapeDtypeStruct((n, WINDOW, KV_W), _F32),
                   jax.ShapeDtypeStruct((n, WINDOW, KV_W), _F32),
                   jax.ShapeDtypeStruct((n, CONV_HIST, D_CONV), _F32)],
        scratch_shapes=[pltpu.VMEM((n * (HIST_PAD + t), D_CONV), _F32),
                        pltpu.VMEM((ntok, D_CONV), _F32),
                        pltpu.VMEM((ntok, ATTN_W), _F32)],
        compiler_params=pltpu.CompilerParams(vmem_limit_bytes=VMEM_LIMIT),
        name="front_sample",
    )(sink, x2d, norm_g, w_front, cw, cb, lg, lb, cache_k, cache_v, state_conv)


def _back(x2d, z, u, norm_g, w_merge, w_pw, w_o, w_out, final_g, t):
    ntok = x2d.shape[0]
    row = lambda width: pl.BlockSpec((t, width), lambda i: (i, 0))
    return pl.pallas_call(
        _back_kernel,
        grid=(ntok // t,),
        in_specs=[row(D_MODEL), row(D_CONV), row(ATTN_W), _VMEM, _VMEM, _VMEM, _VMEM, _VMEM, _VMEM],
        out_specs=row(D_MODEL),
        out_shape=jax.ShapeDtypeStruct((ntok, D_MODEL), _F32),
        scratch_shapes=[pltpu.VMEM((t, D_MODEL), _BF16)],
        compiler_params=pltpu.CompilerParams(dimension_semantics=("arbitrary",),
                                             vmem_limit_bytes=VMEM_LIMIT),
        name="back",
    )(x2d, z, u, norm_g, w_merge, w_pw, w_o, w_out, final_g)


def kernel(x_prompt, x_sample, cache_k, cache_v, state_conv, norm_g, w_in, conv_w, conv_b, ln_g, ln_b,
           w_conv_pw, attn_sink, w_o_attn, w_out, final_g):
    depth = w_in.shape[0]
    assert depth == 1, "single-layer step only"
    bsz, seq, _ = x_prompt.shape
    n, t_s, _ = x_sample.shape
    assert seq % PROMPT_TILE == 0 and PROMPT_TILE % CHUNK == 0 and PROMPT_TILE >= WINDOW

    g = norm_g[0].reshape(1, D_MODEL)
    fg = final_g.reshape(1, D_MODEL)
    w_front = w_in[0, :, :FRONT_COLS].astype(_BF16)
    w_merge = w_in[0, :, FRONT_COLS:].astype(_BF16)
    w_pw = w_conv_pw[0].astype(_BF16)
    w_o = w_o_attn[0].astype(_BF16)
    w_ob = w_out[0].astype(_BF16)
    cw = conv_w[0]
    cb = conv_b[0].reshape(1, D_CONV)
    lg = ln_g[0].reshape(1, D_CONV)
    lb = ln_b[0].reshape(1, D_CONV)
    sink = attn_sink[0]

    xp2d = x_prompt.reshape(bsz * seq, D_MODEL)
    z_p, u_p, kw_p, vw_p, cs_p = _front_prompt(x_prompt, sink, g, w_front, cw, cb, lg, lb)
    y_p = _back(xp2d, z_p, u_p, g, w_merge, w_pw, w_o, w_ob, fg, PROMPT_TILE)

    xs2d = x_sample.reshape(n * t_s, D_MODEL)
    ck = cache_k[0].reshape(n, WINDOW, KV_W)
    cv = cache_v[0].reshape(n, WINDOW, KV_W)
    z_s, u_s, kw_s, vw_s, cs_s = _front_sample(xs2d, sink, g, w_front, cw, cb, lg, lb, ck, cv, state_conv[0])
    y_s = _back(xs2d, z_s, u_s, g, w_merge, w_pw, w_o, w_ob, fg, n * t_s)

    kv_shape = lambda rows: (depth, rows, WINDOW, N_KV_HEADS, HEAD_DIM)
    return (y_p.reshape(bsz, seq, D_MODEL), y_s.reshape(n, t_s, D_MODEL),
            kw_p.reshape(kv_shape(bsz)), vw_p.reshape(kv_shape(bsz)), cs_p.reshape(depth, bsz, CONV_HIST, D_CONV),
            kw_s.reshape(kv_shape(n)), vw_s.reshape(kv_shape(n)), cs_s.reshape(depth, n, CONV_HIST, D_CONV))
```

```python
import functools

import jax
import jax.numpy as jnp
from jax import lax
from jax.experimental import pallas as pl
from jax.experimental.pallas import tpu as pltpu

D_MODEL = 2048
D_CONV = 1024
CONV_WIDTH = 31
CONV_HIST = CONV_WIDTH - 1
N_HEADS = 16
N_KV_HEADS = 2
HEAD_DIM = 64
GROUP = N_HEADS // N_KV_HEADS
ATTN_W = N_HEADS * HEAD_DIM
KV_W = N_KV_HEADS * HEAD_DIM
CHUNK = 64
WINDOW = 128
EPS = 1e-6
NEG = -1e30
SCALE = HEAD_DIM ** -0.5

C_A, C_B, C_GC = 0, D_CONV, 2 * D_CONV
C_Q = 3 * D_CONV
C_KV = C_Q + ATTN_W
C_GA = C_KV + 2 * KV_W
FRONT_COLS = C_GA + ATTN_W
MERGE_COLS = 2 * D_MODEL

LANES = 128
SUBLANES = 8
HIST_PAD = 32
HIST_OFF = HIST_PAD - CONV_HIST
PROMPT_TILE = 256
SEG = PROMPT_TILE // SUBLANES
SEG_PITCH = SEG + 4
STREAM_PITCH = 52
CONV_GROUP = 4
N_SLABS = D_CONV // LANES
VMEM_LIMIT = 56 * 1024 * 1024

_SLOPES = [2.0 ** (-8.0 * (h + 1) / N_HEADS) for h in range(N_HEADS)]

_F32 = jnp.float32
_BF16 = jnp.bfloat16


def _sigmoid(x):
    return 1.0 / (1.0 + jnp.exp(-x))


def _silu(x):
    return x * _sigmoid(x)


def _rms_rows(x, g):
    return x * lax.rsqrt(jnp.mean(x * x, axis=-1, keepdims=True) + EPS) * g


def _dot(a, b):
    return jnp.dot(a, b, preferred_element_type=_F32)


def _conv_strided(p_ref, row_of, stride, n_out, cw_ref, cb_ref, store):
    for c in range(N_SLABS):
        cs = slice(c * LANES, (c + 1) * LANES)
        bias = jnp.broadcast_to(cb_ref[:, cs], (SUBLANES, LANES))
        for m0 in range(0, n_out, CONV_GROUP):
            accs = [bias] * CONV_GROUP
            loaded = {}
            for j in range(CONV_WIDTH):
                w = cw_ref[j:j + 1, cs]
                for g in range(CONV_GROUP):
                    i = m0 + g + j
                    if i not in loaded:
                        loaded[i] = p_ref[c, pl.ds(row_of(i), SUBLANES, stride=stride), :]
                    accs[g] = accs[g] + loaded[i] * w
            for g in range(CONV_GROUP):
                store(c, m0 + g, accs[g])


def _slabs_to_rows(y3_ref):
    return jnp.concatenate([y3_ref[c] for c in range(N_SLABS)], axis=1)


def _conv_gate(y, gc, lg, lb):
    mu = jnp.mean(y, axis=-1, keepdims=True)
    yc = y - mu
    ln = yc * lax.rsqrt(jnp.mean(yc * yc, axis=-1, keepdims=True) + EPS) * lg + lb
    return _silu(ln) * _silu(gc)


def _attend(q, kd, vd, sink_ref, key_pos0):
    tq = q.shape[0]
    nk = kd[0].shape[0]
    lane = lax.broadcasted_iota(jnp.int32, (tq, LANES), 1)
    low_half = lane < HEAD_DIM
    qi = lax.broadcasted_iota(jnp.int32, (tq, nk), 0)
    sj = lax.broadcasted_iota(jnp.int32, (tq, nk), 1)
    dist = jnp.abs(qi + WINDOW - sj).astype(_F32)
    valid = None if key_pos0 is None else (sj + key_pos0) >= 0
    zero = jnp.zeros((), _BF16)
    outs = []
    for kv in range(N_KV_HEADS):
        blocks = []
        for g in range(GROUP):
            h = kv * GROUP + g
            qb = q[:, (h // 2) * LANES:(h // 2 + 1) * LANES]
            blocks.append(jnp.where(low_half if h % 2 == 0 else jnp.logical_not(low_half), qb, zero))
        qs = jnp.concatenate(blocks, axis=0)
        s_all = lax.dot_general(qs, kd[kv], (((1,), (1,)), ((), ())), preferred_element_type=_F32)
        probs, inv = [], []
        for g in range(GROUP):
            h = kv * GROUP + g
            s = s_all[g * tq:(g + 1) * tq] * SCALE - _SLOPES[h] * dist
            if valid is not None:
                s = jnp.where(valid, s, NEG)
            sink = sink_ref[h]
            m = jnp.maximum(jnp.max(s, axis=-1, keepdims=True), sink)
            e = jnp.exp(s - m)
            denom = jnp.sum(e, axis=-1, keepdims=True) + jnp.exp(sink - m)
            probs.append(e.astype(_BF16))
            inv.append(1.0 / denom)
        p_all = jnp.concatenate(probs, axis=0)
        o_all = _dot(p_all, vd[kv]) * jnp.concatenate(inv, axis=0)
        for pair in range(GROUP // 2):
            outs.append(jnp.where(low_half, o_all[(2 * pair) * tq:(2 * pair + 1) * tq],
                                  o_all[(2 * pair + 1) * tq:(2 * pair + 2) * tq]))
    return jnp.concatenate(outs, axis=1)


def _dup_heads(kv_rows):
    kvb = kv_rows.astype(_BF16)
    h0, h1 = kvb[:, :HEAD_DIM], kvb[:, HEAD_DIM:]
    return jnp.concatenate([h0, h0], axis=1), jnp.concatenate([h1, h1], axis=1)


def _front_prompt_kernel(sink_ref, x_ref, g_ref, w_ref, cw_ref, cb_ref, lg_ref, lb_ref,
                         z_ref, u_ref, kwin_ref, vwin_ref, cst_ref,
                         p_ref, y3_ref, k0_ref, k1_ref, v0_ref, v1_ref, q_ref, attn_ref):
    t = x_ref.shape[1]
    j = pl.program_id(1)
    last = pl.num_programs(1) - 1

    @pl.when(j == 0)
    def _():
        p_ref[:, 0:SEG, :] = jnp.zeros((N_SLABS, SEG, LANES), _F32)
        for r in (k0_ref, k1_ref, v0_ref, v1_ref):
            r[0:WINDOW, :] = jnp.zeros((WINDOW, LANES), _BF16)

    h = _rms_rows(x_ref[0], g_ref[...]).astype(_BF16)

    a = _dot(h, w_ref[:, C_A:C_A + D_CONV])
    b = _dot(h, w_ref[:, C_B:C_B + D_CONV])
    glu = a * _sigmoid(b)
    for c in range(N_SLABS):
        for s in range(SUBLANES):
            p_ref[c, (s + 1) * SEG_PITCH:(s + 1) * SEG_PITCH + SEG, :] = glu[s * SEG:(s + 1) * SEG,
                                                                             c * LANES:(c + 1) * LANES]

    def seg_row(i):
        r = i + HIST_OFF
        return r if r < SEG else r - SEG + SEG_PITCH

    def store_y(c, m, y):
        y3_ref[c, pl.ds(m, SUBLANES, stride=SEG), :] = y

    _conv_strided(p_ref, seg_row, SEG_PITCH, SEG, cw_ref, cb_ref, store_y)
    gc = _dot(h, w_ref[:, C_GC:C_GC + D_CONV])
    z_ref[...] = _conv_gate(_slabs_to_rows(y3_ref), gc, lg_ref[...], lb_ref[...]).astype(_BF16)

    @pl.when(j == last)
    def _():
        cst_ref[0] = glu[t - CONV_HIST:, :]

    p_ref[:, 0:SEG, :] = p_ref[:, SUBLANES * SEG_PITCH:SUBLANES * SEG_PITCH + SEG, :]

    q_ref[...] = _dot(h, w_ref[:, C_Q:C_Q + ATTN_W]).astype(_BF16)
    kv = _dot(h, w_ref[:, C_KV:C_KV + 2 * KV_W])
    k_new, v_new = kv[:, :KV_W], kv[:, KV_W:]

    @pl.when(j == last)
    def _():
        kwin_ref[0] = k_new[t - WINDOW:, :]
        vwin_ref[0] = v_new[t - WINDOW:, :]

    kd0, kd1 = _dup_heads(k_new)
    vd0, vd1 = _dup_heads(v_new)
    k0_ref[WINDOW:WINDOW + t, :] = kd0
    k1_ref[WINDOW:WINDOW + t, :] = kd1
    v0_ref[WINDOW:WINDOW + t, :] = vd0
    v1_ref[WINDOW:WINDOW + t, :] = vd1
    nk = WINDOW + CHUNK
    for c in range(t // CHUNK):
        r0 = c * CHUNK
        kd = (k0_ref[r0:r0 + nk, :], k1_ref[r0:r0 + nk, :])
        vd = (v0_ref[r0:r0 + nk, :], v1_ref[r0:r0 + nk, :])
        key_pos0 = j * t + r0 - WINDOW
        attn_ref[r0:r0 + CHUNK, :] = _attend(q_ref[r0:r0 + CHUNK, :], kd, vd, sink_ref, key_pos0)
    for r in (k0_ref, k1_ref, v0_ref, v1_ref):
        r[0:WINDOW, :] = r[t:t + WINDOW, :]
    ga = _dot(h, w_ref[:, C_GA:C_GA + ATTN_W])
    u_ref[...] = (attn_ref[...] * _silu(ga)).astype(_BF16)


def _front_sample_kernel(sink_ref, x_ref, g_ref, w_ref, cw_ref, cb_ref, lg_ref, lb_ref,
                         ck_ref, cv_ref, sc_ref,
                         z_ref, u_ref, kwin_ref, vwin_ref, cst_ref,
                         p_ref, y3_ref, attn_ref):
    n, t = ck_ref.shape[0], x_ref.shape[0] // ck_ref.shape[0]
    h = _rms_rows(x_ref[...], g_ref[...]).astype(_BF16)

    a = _dot(h, w_ref[:, C_A:C_A + D_CONV])
    b = _dot(h, w_ref[:, C_B:C_B + D_CONV])
    glu = a * _sigmoid(b)
    for i in range(n):
        base = i * STREAM_PITCH
        for c in range(N_SLABS):
            cs = slice(c * LANES, (c + 1) * LANES)
            p_ref[c, base + HIST_OFF:base + HIST_PAD, :] = sc_ref[i, :, cs]
            p_ref[c, base + HIST_PAD:base + HIST_PAD + t, :] = glu[i * t:(i + 1) * t, cs]
            cst_ref[i, :, cs] = p_ref[c, base + HIST_PAD + t - CONV_HIST:base + HIST_PAD + t, :]
    for grp in range(n // SUBLANES):

        def store_y(c, m, y, grp=grp):
            y3_ref[c, pl.ds(grp * SUBLANES * t + m, SUBLANES, stride=t), :] = y

        _conv_strided(p_ref, lambda i, grp=grp: grp * SUBLANES * STREAM_PITCH + HIST_OFF + i, STREAM_PITCH, t,
                      cw_ref, cb_ref, store_y)
    gc = _dot(h, w_ref[:, C_GC:C_GC + D_CONV])
    z_ref[...] = _conv_gate(_slabs_to_rows(y3_ref), gc, lg_ref[...], lb_ref[...]).astype(_BF16)

    q = _dot(h, w_ref[:, C_Q:C_Q + ATTN_W]).astype(_BF16)
    kv = _dot(h, w_ref[:, C_KV:C_KV + 2 * KV_W])
    for i in range(n):
        k_all = jnp.concatenate([ck_ref[i], kv[i * t:(i + 1) * t, :KV_W]], axis=0)
        v_all = jnp.concatenate([cv_ref[i], kv[i * t:(i + 1) * t, KV_W:]], axis=0)
        kwin_ref[i] = k_all[t:, :]
        vwin_ref[i] = v_all[t:, :]
        attn_ref[i * t:(i + 1) * t, :] = _attend(q[i * t:(i + 1) * t, :], _dup_heads(k_all), _dup_heads(v_all),
                                                 sink_ref, None)
    ga = _dot(h, w_ref[:, C_GA:C_GA + ATTN_W])
    u_ref[...] = (attn_ref[...] * _silu(ga)).astype(_BF16)


def _back_kernel(x_ref, z_ref, u_ref, g_ref, wm_ref, wpw_ref, wo_ref, wout_ref, fg_ref, y_ref, m_ref):
    x = x_ref[...]
    h = _rms_rows(x, g_ref[...]).astype(_BF16)
    z = z_ref[...]
    u = u_ref[...]
    nb = 512
    for c in range(D_MODEL // nb):
        cs = slice(c * nb, (c + 1) * nb)
        mc = _dot(h, wm_ref[:, c * nb:(c + 1) * nb])
        ma = _dot(h, wm_ref[:, D_MODEL + c * nb:D_MODEL + (c + 1) * nb])
        conv_o = _dot(z, wpw_ref[:, cs])
        attn_o = _dot(u, wo_ref[:, cs])
        m_ref[:, cs] = (_sigmoid(mc) * conv_o + _sigmoid(ma) * attn_o).astype(_BF16)
    y_ref[...] = _rms_rows(x + _dot(m_ref[...], wout_ref[...]), fg_ref[...])


_VMEM = pl.BlockSpec(memory_space=pltpu.VMEM)
_SMEM = pl.BlockSpec(memory_space=pltpu.SMEM)


def _front_prompt(x, sink, norm_g, w_front, cw, cb, lg, lb):
    bsz, seq, _ = x.shape
    t = PROMPT_TILE
    nt = seq // t
    tok = lambda width, dt: jax.ShapeDtypeStruct((bsz * seq, width), dt)
    return pl.pallas_call(
        _front_prompt_kernel,
        grid=(bsz, nt),
        in_specs=[_SMEM, pl.BlockSpec((1, t, D_MODEL), lambda b, j: (b, j, 0)),
                  _VMEM, _VMEM, _VMEM, _VMEM, _VMEM, _VMEM],
        out_specs=[pl.BlockSpec((t, D_CONV), lambda b, j: (b * nt + j, 0)),
                   pl.BlockSpec((t, ATTN_W), lambda b, j: (b * nt + j, 0)),
                   pl.BlockSpec((1, WINDOW, KV_W), lambda b, j: (b, 0, 0)),
                   pl.BlockSpec((1, WINDOW, KV_W), lambda b, j: (b, 0, 0)),
                   pl.BlockSpec((1, CONV_HIST, D_CONV), lambda b, j: (b, 0, 0))],
        out_shape=[tok(D_CONV, _BF16), tok(ATTN_W, _BF16),
                   jax.ShapeDtypeStruct((bsz, WINDOW, KV_W), _F32),
                   jax.ShapeDtypeStruct((bsz, WINDOW, KV_W), _F32),
                   jax.ShapeDtypeStruct((bsz, CONV_HIST, D_CONV), _F32)],
        scratch_shapes=[pltpu.VMEM((N_SLABS, (SUBLANES + 1) * SEG_PITCH, LANES), _F32),
                        pltpu.VMEM((N_SLABS, t, LANES), _F32),
                        pltpu.VMEM((WINDOW + t, LANES), _BF16),
                        pltpu.VMEM((WINDOW + t, LANES), _BF16),
                        pltpu.VMEM((WINDOW + t, LANES), _BF16),
                        pltpu.VMEM((WINDOW + t, LANES), _BF16),
                        pltpu.VMEM((t, ATTN_W), _BF16),
                        pltpu.VMEM((t, ATTN_W), _F32)],
        compiler_params=pltpu.CompilerParams(dimension_semantics=("arbitrary", "arbitrary"),
                                             vmem_limit_bytes=VMEM_LIMIT),
        name="front_prompt",
    )(sink, x, norm_g, w_front, cw, cb, lg, lb)


def _front_sample(x2d, sink, norm_g, w_front, cw, cb, lg, lb, cache_k, cache_v, state_conv):
    n = cache_k.shape[0]
    ntok = x2d.shape[0]
    t = ntok // n
    return pl.pallas_call(
        _front_sample_kernel,
        in_specs=[_SMEM] + [_VMEM] * 10,
        out_specs=[_VMEM] * 5,
        out_shape=[jax.ShapeDtypeStruct((ntok, D_CONV), _BF16),
                   jax.ShapeDtypeStruct((ntok, ATTN_W), _BF16),
                   jax.ShapeDtypeStruct((n, WINDOW, KV_W), _F32),
                   jax.ShapeDtypeStruct((n, WINDOW, KV_W), _F32),
                   jax.ShapeDtypeStruct((n, CONV_HIST, D_CONV), _F32)],
        scratch_shapes=[pltpu.VMEM((N_SLABS, n * STREAM_PITCH, LANES), _F32),
                        pltpu.VMEM((N_SLABS, ntok, LANES), _F32),
                        pltpu.VMEM((ntok, ATTN_W), _F32)],
        compiler_params=pltpu.CompilerParams(vmem_limit_bytes=VMEM_LIMIT),
        name="front_sample",
    )(sink, x2d, norm_g, w_front, cw, cb, lg, lb, cache_k, cache_v, state_conv)


def _back(x2d, z, u, norm_g, w_merge, w_pw, w_o, w_out, final_g, t):
    ntok = x2d.shape[0]
    row = lambda width: pl.BlockSpec((t, width), lambda i: (i, 0))
    return pl.pallas_call(
        _back_kernel,
        grid=(ntok // t,),
        in_specs=[row(D_MODEL), row(D_CONV), row(ATTN_W), _VMEM, _VMEM, _VMEM, _VMEM, _VMEM, _VMEM],
        out_specs=row(D_MODEL),
        out_shape=jax.ShapeDtypeStruct((ntok, D_MODEL), _F32),
        scratch_shapes=[pltpu.VMEM((t, D_MODEL), _BF16)],
        compiler_params=pltpu.CompilerParams(dimension_semantics=("arbitrary",),
                                             vmem_limit_bytes=VMEM_LIMIT),
        name="back",
    )(x2d, z, u, norm_g, w_merge, w_pw, w_o, w_out, final_g)


def kernel(x_prompt, x_sample, cache_k, cache_v, state_conv, norm_g, w_in, conv_w, conv_b, ln_g, ln_b,
           w_conv_pw, attn_sink, w_o_attn, w_out, final_g):
    depth = w_in.shape[0]
    assert depth == 1, "single-layer step only"
    bsz, seq, _ = x_prompt.shape
    n, t_s, _ = x_sample.shape
    assert seq % PROMPT_TILE == 0 and PROMPT_TILE % CHUNK == 0 and PROMPT_TILE >= WINDOW
    assert SEG >= HIST_PAD and SEG % CONV_GROUP == 0
    assert n % SUBLANES == 0 and t_s % CONV_GROUP == 0 and HIST_PAD + t_s <= STREAM_PITCH

    g = norm_g[0].reshape(1, D_MODEL)
    fg = final_g.reshape(1, D_MODEL)
    w_front = w_in[0, :, :FRONT_COLS].astype(_BF16)
    w_merge = w_in[0, :, FRONT_COLS:].astype(_BF16)
    w_pw = w_conv_pw[0].astype(_BF16)
    w_o = w_o_attn[0].astype(_BF16)
    w_ob = w_out[0].astype(_BF16)
    cw = conv_w[0]
    cb = conv_b[0].reshape(1, D_CONV)
    lg = ln_g[0].reshape(1, D_CONV)
    lb = ln_b[0].reshape(1, D_CONV)
    sink = attn_sink[0]

    xp2d = x_prompt.reshape(bsz * seq, D_MODEL)
    z_p, u_p, kw_p, vw_p, cs_p = _front_prompt(x_prompt, sink, g, w_front, cw, cb, lg, lb)
    y_p = _back(xp2d, z_p, u_p, g, w_merge, w_pw, w_o, w_ob, fg, PROMPT_TILE)

    xs2d = x_sample.reshape(n * t_s, D_MODEL)
    ck = cache_k[0].reshape(n, WINDOW, KV_W)
    cv = cache_v[0].reshape(n, WINDOW, KV_W)
    z_s, u_s, kw_s, vw_s, cs_s = _front_sample(xs2d, sink, g, w_front, cw, cb, lg, lb, ck, cv, state_conv[0])
    y_s = _back(xs2d, z_s, u_s, g, w_merge, w_pw, w_o, w_ob, fg, n * t_s)

    kv_shape = lambda rows: (depth, rows, WINDOW, N_KV_HEADS, HEAD_DIM)
    return (y_p.reshape(bsz, seq, D_MODEL), y_s.reshape(n, t_s, D_MODEL),
            kw_p.reshape(kv_shape(bsz)), vw_p.reshape(kv_shape(bsz)), cs_p.reshape(depth, bsz, CONV_HIST, D_CONV),
            kw_s.reshape(kv_shape(n)), vw_s.reshape(kv_shape(n)), cs_s.reshape(depth, n, CONV_HIST, D_CONV))
```

```python
import jax
import jax.numpy as jnp
from jax import lax
from jax.experimental import pallas as pl
from jax.experimental.pallas import tpu as pltpu

D_MODEL = 2048
D_CONV = 1024
CONV_WIDTH = 31
CONV_HIST = CONV_WIDTH - 1
N_HEADS = 16
N_KV_HEADS = 2
HEAD_DIM = 64
GROUP = N_HEADS // N_KV_HEADS
ATTN_W = N_HEADS * HEAD_DIM
KV_W = N_KV_HEADS * HEAD_DIM
CHUNK = 64
WINDOW = 128
EPS = 1e-6
NEG = -1e30
SCALE = HEAD_DIM ** -0.5

C_A, C_B, C_GC = 0, D_CONV, 2 * D_CONV
C_Q = 3 * D_CONV
C_KV = C_Q + ATTN_W
C_GA = C_KV + 2 * KV_W
FRONT_COLS = C_GA + ATTN_W
MERGE_COLS = 2 * D_MODEL

LANES = 128
SUBLANES = 8
HIST_PAD = 32
HIST_OFF = HIST_PAD - CONV_HIST
PROMPT_TILE = 512
BACK_TILE = 256
CAST_ROWS = 256
SEG = PROMPT_TILE // SUBLANES
SEG_PITCH = SEG + 4
STREAM_PITCH = 52
CONV_GROUP = 4
N_SLABS = D_CONV // LANES
VMEM_LIMIT = 58 * 1024 * 1024

_SLOPES = [2.0 ** (-8.0 * (h + 1) / N_HEADS) for h in range(N_HEADS)]

_F32 = jnp.float32
_BF16 = jnp.bfloat16


def _sigmoid(x):
    return 1.0 / (1.0 + jnp.exp(-x))


def _silu(x):
    return x * _sigmoid(x)


def _rms_rows(x, g):
    return x * lax.rsqrt(jnp.mean(x * x, axis=-1, keepdims=True) + EPS) * g


def _dot(a, b):
    return jnp.dot(a, b, preferred_element_type=_F32)


def _conv_strided(p_ref, row_of, stride, n_out, cw_ref, cb_ref, store):
    for c in range(N_SLABS):
        cs = slice(c * LANES, (c + 1) * LANES)
        bias = jnp.broadcast_to(cb_ref[:, cs], (SUBLANES, LANES))
        for m0 in range(0, n_out, CONV_GROUP):
            accs = [bias] * CONV_GROUP
            loaded = {}
            for j in range(CONV_WIDTH):
                w = cw_ref[j:j + 1, cs]
                for g in range(CONV_GROUP):
                    i = m0 + g + j
                    if i not in loaded:
                        loaded[i] = p_ref[c, pl.ds(row_of(i), SUBLANES, stride=stride), :]
                    accs[g] = accs[g] + loaded[i] * w
            for g in range(CONV_GROUP):
                store(c, m0 + g, accs[g])


def _slabs_to_rows(y3_ref):
    return jnp.concatenate([y3_ref[c] for c in range(N_SLABS)], axis=1)


def _conv_gate(y, gc, lg, lb):
    mu = jnp.mean(y, axis=-1, keepdims=True)
    yc = y - mu
    ln = yc * lax.rsqrt(jnp.mean(yc * yc, axis=-1, keepdims=True) + EPS) * lg + lb
    return _silu(ln) * _silu(gc)


def _attend(q, kd, vd, sink_ref, key_pos0):
    tq = q.shape[0]
    nk = kd[0].shape[0]
    lane = lax.broadcasted_iota(jnp.int32, (tq, LANES), 1)
    low_half = lane < HEAD_DIM
    qi = lax.broadcasted_iota(jnp.int32, (tq, nk), 0)
    sj = lax.broadcasted_iota(jnp.int32, (tq, nk), 1)
    dist = jnp.abs(qi + WINDOW - sj).astype(_F32)
    valid = None if key_pos0 is None else (sj + key_pos0) >= 0
    zero = jnp.zeros((), _BF16)
    outs = []
    for kv in range(N_KV_HEADS):
        blocks = []
        for g in range(GROUP):
            h = kv * GROUP + g
            qb = q[:, (h // 2) * LANES:(h // 2 + 1) * LANES]
            blocks.append(jnp.where(low_half if h % 2 == 0 else jnp.logical_not(low_half), qb, zero))
        qs = jnp.concatenate(blocks, axis=0)
        s_all = lax.dot_general(qs, kd[kv], (((1,), (1,)), ((), ())), preferred_element_type=_F32)
        probs, inv = [], []
        for g in range(GROUP):
            h = kv * GROUP + g
            s = s_all[g * tq:(g + 1) * tq] * SCALE - _SLOPES[h] * dist
            if valid is not None:
                s = jnp.where(valid, s, NEG)
            sink = sink_ref[h]
            m = jnp.maximum(jnp.max(s, axis=-1, keepdims=True), sink)
            e = jnp.exp(s - m)
            denom = jnp.sum(e, axis=-1, keepdims=True) + jnp.exp(sink - m)
            probs.append(e.astype(_BF16))
            inv.append(1.0 / denom)
        p_all = jnp.concatenate(probs, axis=0)
        o_all = _dot(p_all, vd[kv]) * jnp.concatenate(inv, axis=0)
        for pair in range(GROUP // 2):
            outs.append(jnp.where(low_half, o_all[(2 * pair) * tq:(2 * pair + 1) * tq],
                                  o_all[(2 * pair + 1) * tq:(2 * pair + 2) * tq]))
    return jnp.concatenate(outs, axis=1)


def _dup_heads(kv_rows):
    kvb = kv_rows.astype(_BF16)
    h0, h1 = kvb[:, :HEAD_DIM], kvb[:, HEAD_DIM:]
    return jnp.concatenate([h0, h0], axis=1), jnp.concatenate([h1, h1], axis=1)


def _front_prompt_kernel(sink_ref, x_ref, g_ref, w_ref, cw_ref, cb_ref, lg_ref, lb_ref,
                         z_ref, u_ref, kwin_ref, vwin_ref, cst_ref,
                         p_ref, y3_ref, k0_ref, k1_ref, v0_ref, v1_ref, q_ref, attn_ref):
    t = x_ref.shape[1]
    j = pl.program_id(1)
    last = pl.num_programs(1) - 1

    @pl.when(j == 0)
    def _():
        p_ref[:, 0:SEG, :] = jnp.zeros((N_SLABS, SEG, LANES), _F32)
        for r in (k0_ref, k1_ref, v0_ref, v1_ref):
            r[0:WINDOW, :] = jnp.zeros((WINDOW, LANES), _BF16)

    h = _rms_rows(x_ref[0], g_ref[...]).astype(_BF16)

    a = _dot(h, w_ref[:, C_A:C_A + D_CONV])
    b = _dot(h, w_ref[:, C_B:C_B + D_CONV])
    glu = a * _sigmoid(b)
    for c in range(N_SLABS):
        for s in range(SUBLANES):
            p_ref[c, (s + 1) * SEG_PITCH:(s + 1) * SEG_PITCH + SEG, :] = glu[s * SEG:(s + 1) * SEG,
                                                                             c * LANES:(c + 1) * LANES]

    def seg_row(i):
        r = i + SEG - CONV_HIST
        return r if r < SEG else r - SEG + SEG_PITCH

    def store_y(c, m, y):
        y3_ref[c, pl.ds(m, SUBLANES, stride=SEG), :] = y

    _conv_strided(p_ref, seg_row, SEG_PITCH, SEG, cw_ref, cb_ref, store_y)
    gc = _dot(h, w_ref[:, C_GC:C_GC + D_CONV])
    z_ref[...] = _conv_gate(_slabs_to_rows(y3_ref), gc, lg_ref[...], lb_ref[...]).astype(_BF16)

    @pl.when(j == last)
    def _():
        cst_ref[0] = glu[t - CONV_HIST:, :]

    p_ref[:, 0:SEG, :] = p_ref[:, SUBLANES * SEG_PITCH:SUBLANES * SEG_PITCH + SEG, :]

    q_ref[...] = _dot(h, w_ref[:, C_Q:C_Q + ATTN_W]).astype(_BF16)
    kv = _dot(h, w_ref[:, C_KV:C_KV + 2 * KV_W])
    k_new, v_new = kv[:, :KV_W], kv[:, KV_W:]

    @pl.when(j == last)
    def _():
        kwin_ref[0] = k_new[t - WINDOW:, :]
        vwin_ref[0] = v_new[t - WINDOW:, :]

    kd0, kd1 = _dup_heads(k_new)
    vd0, vd1 = _dup_heads(v_new)
    k0_ref[WINDOW:WINDOW + t, :] = kd0
    k1_ref[WINDOW:WINDOW + t, :] = kd1
    v0_ref[WINDOW:WINDOW + t, :] = vd0
    v1_ref[WINDOW:WINDOW + t, :] = vd1
    nk = WINDOW + CHUNK
    for c in range(t // CHUNK):
        r0 = c * CHUNK
        kd = (k0_ref[r0:r0 + nk, :], k1_ref[r0:r0 + nk, :])
        vd = (v0_ref[r0:r0 + nk, :], v1_ref[r0:r0 + nk, :])
        key_pos0 = j * t + r0 - WINDOW
        attn_ref[r0:r0 + CHUNK, :] = _attend(q_ref[r0:r0 + CHUNK, :], kd, vd, sink_ref, key_pos0)
    for r in (k0_ref, k1_ref, v0_ref, v1_ref):
        r[0:WINDOW, :] = r[t:t + WINDOW, :]
    ga = _dot(h, w_ref[:, C_GA:C_GA + ATTN_W])
    u_ref[...] = (attn_ref[...] * _silu(ga)).astype(_BF16)


def _front_sample_kernel(sink_ref, x_ref, g_ref, w_ref, cw_ref, cb_ref, lg_ref, lb_ref,
                         ck_ref, cv_ref, sc_ref,
                         z_ref, u_ref, kwin_ref, vwin_ref, cst_ref,
                         p_ref, y3_ref, attn_ref):
    n, t = ck_ref.shape[0], x_ref.shape[0] // ck_ref.shape[0]
    h = _rms_rows(x_ref[...], g_ref[...]).astype(_BF16)

    a = _dot(h, w_ref[:, C_A:C_A + D_CONV])
    b = _dot(h, w_ref[:, C_B:C_B + D_CONV])
    glu = a * _sigmoid(b)
    for i in range(n):
        base = i * STREAM_PITCH
        for c in range(N_SLABS):
            cs = slice(c * LANES, (c + 1) * LANES)
            p_ref[c, base + HIST_OFF:base + HIST_PAD, :] = sc_ref[i, :, cs]
            p_ref[c, base + HIST_PAD:base + HIST_PAD + t, :] = glu[i * t:(i + 1) * t, cs]
            cst_ref[i, :, cs] = p_ref[c, base + HIST_PAD + t - CONV_HIST:base + HIST_PAD + t, :]
    for grp in range(n // SUBLANES):

        def store_y(c, m, y, grp=grp):
            y3_ref[c, pl.ds(grp * SUBLANES * t + m, SUBLANES, stride=t), :] = y

        _conv_strided(p_ref, lambda i, grp=grp: grp * SUBLANES * STREAM_PITCH + HIST_OFF + i, STREAM_PITCH, t,
                      cw_ref, cb_ref, store_y)
    gc = _dot(h, w_ref[:, C_GC:C_GC + D_CONV])
    z_ref[...] = _conv_gate(_slabs_to_rows(y3_ref), gc, lg_ref[...], lb_ref[...]).astype(_BF16)

    q = _dot(h, w_ref[:, C_Q:C_Q + ATTN_W]).astype(_BF16)
    kv = _dot(h, w_ref[:, C_KV:C_KV + 2 * KV_W])
    for i in range(n):
        k_all = jnp.concatenate([ck_ref[i], kv[i * t:(i + 1) * t, :KV_W]], axis=0)
        v_all = jnp.concatenate([cv_ref[i], kv[i * t:(i + 1) * t, KV_W:]], axis=0)
        kwin_ref[i] = k_all[t:, :]
        vwin_ref[i] = v_all[t:, :]
        attn_ref[i * t:(i + 1) * t, :] = _attend(q[i * t:(i + 1) * t, :], _dup_heads(k_all), _dup_heads(v_all),
                                                 sink_ref, None)
    ga = _dot(h, w_ref[:, C_GA:C_GA + ATTN_W])
    u_ref[...] = (attn_ref[...] * _silu(ga)).astype(_BF16)


def _back_kernel(x_ref, z_ref, u_ref, g_ref, wm_ref, wpw_ref, wo_ref, wout_ref, fg_ref, y_ref, m_ref):
    x = x_ref[...]
    h = _rms_rows(x, g_ref[...]).astype(_BF16)
    z = z_ref[...]
    u = u_ref[...]
    nb = 512
    for c in range(D_MODEL // nb):
        cs = slice(c * nb, (c + 1) * nb)
        mc = _dot(h, wm_ref[:, c * nb:(c + 1) * nb])
        ma = _dot(h, wm_ref[:, D_MODEL + c * nb:D_MODEL + (c + 1) * nb])
        conv_o = _dot(z, wpw_ref[:, cs])
        attn_o = _dot(u, wo_ref[:, cs])
        m_ref[:, cs] = (_sigmoid(mc) * conv_o + _sigmoid(ma) * attn_o).astype(_BF16)
    y_ref[...] = _rms_rows(x + _dot(m_ref[...], wout_ref[...]), fg_ref[...])


def _cast_kernel(w_ref, wf_ref, wm_ref):
    wf_ref[...] = w_ref[0, :, :FRONT_COLS].astype(_BF16)
    wm_ref[...] = w_ref[0, :, FRONT_COLS:].astype(_BF16)


_VMEM = pl.BlockSpec(memory_space=pltpu.VMEM)
_SMEM = pl.BlockSpec(memory_space=pltpu.SMEM)


def _cast_w_in(w_in):
    rows, cols = w_in.shape[1:]
    return pl.pallas_call(
        _cast_kernel,
        grid=(rows // CAST_ROWS,),
        in_specs=[pl.BlockSpec((1, CAST_ROWS, cols), lambda i: (0, i, 0))],
        out_specs=[pl.BlockSpec((CAST_ROWS, FRONT_COLS), lambda i: (i, 0)),
                   pl.BlockSpec((CAST_ROWS, cols - FRONT_COLS), lambda i: (i, 0))],
        out_shape=[jax.ShapeDtypeStruct((rows, FRONT_COLS), _BF16),
                   jax.ShapeDtypeStruct((rows, cols - FRONT_COLS), _BF16)],
        compiler_params=pltpu.CompilerParams(dimension_semantics=("arbitrary",), vmem_limit_bytes=VMEM_LIMIT),
        name="cast_w_in",
    )(w_in)


def _front_prompt(x, sink, norm_g, w_front, cw, cb, lg, lb):
    bsz, seq, _ = x.shape
    t = PROMPT_TILE
    nt = seq // t
    tok = lambda width, dt: jax.ShapeDtypeStruct((bsz * seq, width), dt)
    return pl.pallas_call(
        _front_prompt_kernel,
        grid=(bsz, nt),
        in_specs=[_SMEM, pl.BlockSpec((1, t, D_MODEL), lambda b, j: (b, j, 0)),
                  _VMEM, _VMEM, _VMEM, _VMEM, _VMEM, _VMEM],
        out_specs=[pl.BlockSpec((t, D_CONV), lambda b, j: (b * nt + j, 0)),
                   pl.BlockSpec((t, ATTN_W), lambda b, j: (b * nt + j, 0)),
                   pl.BlockSpec((1, WINDOW, KV_W), lambda b, j: (b, 0, 0)),
                   pl.BlockSpec((1, WINDOW, KV_W), lambda b, j: (b, 0, 0)),
                   pl.BlockSpec((1, CONV_HIST, D_CONV), lambda b, j: (b, 0, 0))],
        out_shape=[tok(D_CONV, _BF16), tok(ATTN_W, _BF16),
                   jax.ShapeDtypeStruct((bsz, WINDOW, KV_W), _F32),
                   jax.ShapeDtypeStruct((bsz, WINDOW, KV_W), _F32),
                   jax.ShapeDtypeStruct((bsz, CONV_HIST, D_CONV), _F32)],
        scratch_shapes=[pltpu.VMEM((N_SLABS, (SUBLANES + 1) * SEG_PITCH, LANES), _F32),
                        pltpu.VMEM((N_SLABS, t, LANES), _F32),
                        pltpu.VMEM((WINDOW + t, LANES), _BF16),
                        pltpu.VMEM((WINDOW + t, LANES), _BF16),
                        pltpu.VMEM((WINDOW + t, LANES), _BF16),
                        pltpu.VMEM((WINDOW + t, LANES), _BF16),
                        pltpu.VMEM((t, ATTN_W), _BF16),
                        pltpu.VMEM((t, ATTN_W), _F32)],
        compiler_params=pltpu.CompilerParams(dimension_semantics=("arbitrary", "arbitrary"),
                                             vmem_limit_bytes=VMEM_LIMIT),
        name="front_prompt",
    )(sink, x, norm_g, w_front, cw, cb, lg, lb)


def _front_sample(x2d, sink, norm_g, w_front, cw, cb, lg, lb, cache_k, cache_v, state_conv):
    n = cache_k.shape[0]
    ntok = x2d.shape[0]
    t = ntok // n
    return pl.pallas_call(
        _front_sample_kernel,
        in_specs=[_SMEM] + [_VMEM] * 10,
        out_specs=[_VMEM] * 5,
        out_shape=[jax.ShapeDtypeStruct((ntok, D_CONV), _BF16),
                   jax.ShapeDtypeStruct((ntok, ATTN_W), _BF16),
                   jax.ShapeDtypeStruct((n, WINDOW, KV_W), _F32),
                   jax.ShapeDtypeStruct((n, WINDOW, KV_W), _F32),
                   jax.ShapeDtypeStruct((n, CONV_HIST, D_CONV), _F32)],
        scratch_shapes=[pltpu.VMEM((N_SLABS, n * STREAM_PITCH, LANES), _F32),
                        pltpu.VMEM((N_SLABS, ntok, LANES), _F32),
                        pltpu.VMEM((ntok, ATTN_W), _F32)],
        compiler_params=pltpu.CompilerParams(vmem_limit_bytes=VMEM_LIMIT),
        name="front_sample",
    )(sink, x2d, norm_g, w_front, cw, cb, lg, lb, cache_k, cache_v, state_conv)


def _back(x2d, z, u, norm_g, w_merge, w_pw, w_o, w_out, final_g, t):
    ntok = x2d.shape[0]
    row = lambda width: pl.BlockSpec((t, width), lambda i: (i, 0))
    return pl.pallas_call(
        _back_kernel,
        grid=(ntok // t,),
        in_specs=[row(D_MODEL), row(D_CONV), row(ATTN_W), _VMEM, _VMEM, _VMEM, _VMEM, _VMEM, _VMEM],
        out_specs=row(D_MODEL),
        out_shape=jax.ShapeDtypeStruct((ntok, D_MODEL), _F32),
        scratch_shapes=[pltpu.VMEM((t, D_MODEL), _BF16)],
        compiler_params=pltpu.CompilerParams(dimension_semantics=("arbitrary",),
                                             vmem_limit_bytes=VMEM_LIMIT),
        name="back",
    )(x2d, z, u, norm_g, w_merge, w_pw, w_o, w_out, final_g)


def kernel(x_prompt, x_sample, cache_k, cache_v, state_conv, norm_g, w_in, conv_w, conv_b, ln_g, ln_b,
           w_conv_pw, attn_sink, w_o_attn, w_out, final_g):
    depth = w_in.shape[0]
    assert depth == 1, "single-layer step only"
    bsz, seq, _ = x_prompt.shape
    n, t_s, _ = x_sample.shape
    assert seq % PROMPT_TILE == 0 and PROMPT_TILE % CHUNK == 0 and PROMPT_TILE >= WINDOW
    assert SEG >= HIST_PAD and SEG % CONV_GROUP == 0 and (bsz * seq) % BACK_TILE == 0
    assert w_in.shape[1] % CAST_ROWS == 0 and w_in.shape[2] == FRONT_COLS + MERGE_COLS
    assert n % SUBLANES == 0 and t_s % CONV_GROUP == 0 and HIST_PAD + t_s <= STREAM_PITCH

    g = norm_g[0].reshape(1, D_MODEL)
    fg = final_g.reshape(1, D_MODEL)
    w_front, w_merge = _cast_w_in(w_in)
    w_pw = w_conv_pw[0].astype(_BF16)
    w_o = w_o_attn[0].astype(_BF16)
    w_ob = w_out[0].astype(_BF16)
    cw = conv_w[0]
    cb = conv_b[0].reshape(1, D_CONV)
    lg = ln_g[0].reshape(1, D_CONV)
    lb = ln_b[0].reshape(1, D_CONV)
    sink = attn_sink[0]

    xp2d = x_prompt.reshape(bsz * seq, D_MODEL)
    z_p, u_p, kw_p, vw_p, cs_p = _front_prompt(x_prompt, sink, g, w_front, cw, cb, lg, lb)
    y_p = _back(xp2d, z_p, u_p, g, w_merge, w_pw, w_o, w_ob, fg, BACK_TILE)

    xs2d = x_sample.reshape(n * t_s, D_MODEL)
    ck = cache_k[0].reshape(n, WINDOW, KV_W)
    cv = cache_v[0].reshape(n, WINDOW, KV_W)
    z_s, u_s, kw_s, vw_s, cs_s = _front_sample(xs2d, sink, g, w_front, cw, cb, lg, lb, ck, cv, state_conv[0])
    y_s = _back(xs2d, z_s, u_s, g, w_merge, w_pw, w_o, w_ob, fg, n * t_s)

    kv_shape = lambda rows: (depth, rows, WINDOW, N_KV_HEADS, HEAD_DIM)
    return (y_p.reshape(bsz, seq, D_MODEL), y_s.reshape(n, t_s, D_MODEL),
            kw_p.reshape(kv_shape(bsz)), vw_p.reshape(kv_shape(bsz)), cs_p.reshape(depth, bsz, CONV_HIST, D_CONV),
            kw_s.reshape(kv_shape(n)), vw_s.reshape(kv_shape(n)), cs_s.reshape(depth, n, CONV_HIST, D_CONV))
```

```python
import jax
import jax.numpy as jnp
from jax import lax
from jax.experimental import pallas as pl
from jax.experimental.pallas import tpu as pltpu

D_MODEL = 2048
D_CONV = 1024
CONV_WIDTH = 31
CONV_HIST = CONV_WIDTH - 1
N_HEADS = 16
N_KV_HEADS = 2
HEAD_DIM = 64
GROUP = N_HEADS // N_KV_HEADS
ATTN_W = N_HEADS * HEAD_DIM
KV_W = N_KV_HEADS * HEAD_DIM
CHUNK = 64
WINDOW = 128
EPS = 1e-6
NEG = -1e30
SCALE = HEAD_DIM ** -0.5

C_A, C_B, C_GC = 0, D_CONV, 2 * D_CONV
C_Q = 3 * D_CONV
C_KV = C_Q + ATTN_W
C_GA = C_KV + 2 * KV_W
FRONT_COLS = C_GA + ATTN_W
MERGE_COLS = 2 * D_MODEL

LANES = 128
SUBLANES = 8
HIST_PAD = 32
HIST_OFF = HIST_PAD - CONV_HIST
PROMPT_TILE = 512
BACK_TILE = 256
CAST_ROWS = 256
SEG = PROMPT_TILE // SUBLANES
SEG_PITCH = SEG + 4
STREAM_PITCH = 52
CONV_GROUP = 4
N_SLABS = D_CONV // LANES
PROJ_BLK = 2 * LANES
VMEM_LIMIT = 58 * 1024 * 1024

_SLOPES = [2.0 ** (-8.0 * (h + 1) / N_HEADS) for h in range(N_HEADS)]

_F32 = jnp.float32
_BF16 = jnp.bfloat16


def _sigmoid(x):
    return 1.0 / (1.0 + jnp.exp(-x))


def _silu(x):
    return x * _sigmoid(x)


def _rms_rows(x, g):
    return x * lax.rsqrt(jnp.mean(x * x, axis=-1, keepdims=True) + EPS) * g


def _dot(a, b):
    return jnp.dot(a, b, preferred_element_type=_F32)


def _conv_strided(p_ref, row_of, stride, n_out, cw_ref, cb_ref, store, slabs=None):
    for c in (range(N_SLABS) if slabs is None else slabs):
        cs = slice(c * LANES, (c + 1) * LANES)
        bias = jnp.broadcast_to(cb_ref[:, cs], (SUBLANES, LANES))
        for m0 in range(0, n_out, CONV_GROUP):
            accs = [bias] * CONV_GROUP
            loaded = {}
            for j in range(CONV_WIDTH):
                w = cw_ref[j:j + 1, cs]
                for g in range(CONV_GROUP):
                    i = m0 + g + j
                    if i not in loaded:
                        loaded[i] = p_ref[c, pl.ds(row_of(i), SUBLANES, stride=stride), :]
                    accs[g] = accs[g] + loaded[i] * w
            for g in range(CONV_GROUP):
                store(c, m0 + g, accs[g])


def _slabs_to_rows(y3_ref, slabs=None):
    return jnp.concatenate([y3_ref[c] for c in (range(N_SLABS) if slabs is None else slabs)], axis=1)


def _conv_gate(y, gc, lg, lb):
    mu = jnp.mean(y, axis=-1, keepdims=True)
    yc = y - mu
    ln = yc * lax.rsqrt(jnp.mean(yc * yc, axis=-1, keepdims=True) + EPS) * lg + lb
    return _silu(ln) * _silu(gc)


def _attend(q, kd, vd, sink_ref, key_pos0):
    tq = q.shape[0]
    nk = kd[0].shape[0]
    lane = lax.broadcasted_iota(jnp.int32, (tq, LANES), 1)
    low_half = lane < HEAD_DIM
    qi = lax.broadcasted_iota(jnp.int32, (tq, nk), 0)
    sj = lax.broadcasted_iota(jnp.int32, (tq, nk), 1)
    dist = jnp.abs(qi + WINDOW - sj).astype(_F32)
    valid = None if key_pos0 is None else (sj + key_pos0) >= 0
    zero = jnp.zeros((), _BF16)
    outs = []
    for kv in range(N_KV_HEADS):
        blocks = []
        for g in range(GROUP):
            h = kv * GROUP + g
            qb = q[:, (h // 2) * LANES:(h // 2 + 1) * LANES]
            blocks.append(jnp.where(low_half if h % 2 == 0 else jnp.logical_not(low_half), qb, zero))
        qs = jnp.concatenate(blocks, axis=0)
        s_all = lax.dot_general(qs, kd[kv], (((1,), (1,)), ((), ())), preferred_element_type=_F32)
        probs, inv = [], []
        for g in range(GROUP):
            h = kv * GROUP + g
            s = s_all[g * tq:(g + 1) * tq] - _SLOPES[h] * dist
            if valid is not None:
                s = jnp.where(valid, s, NEG)
            sink = sink_ref[h]
            m = jnp.maximum(jnp.max(s, axis=-1, keepdims=True), sink)
            e = jnp.exp(s - m)
            denom = jnp.sum(e, axis=-1, keepdims=True) + jnp.exp(sink - m)
            probs.append(e.astype(_BF16))
            inv.append(1.0 / denom)
        p_all = jnp.concatenate(probs, axis=0)
        o_all = _dot(p_all, vd[kv]) * jnp.concatenate(inv, axis=0)
        for pair in range(GROUP // 2):
            outs.append(jnp.where(low_half, o_all[(2 * pair) * tq:(2 * pair + 1) * tq],
                                  o_all[(2 * pair + 1) * tq:(2 * pair + 2) * tq]))
    return jnp.concatenate(outs, axis=1)


def _dup_heads(kv_rows):
    kvb = kv_rows.astype(_BF16)
    h0, h1 = kvb[:, :HEAD_DIM], kvb[:, HEAD_DIM:]
    return jnp.concatenate([h0, h0], axis=1), jnp.concatenate([h1, h1], axis=1)


def _front_prompt_kernel(sink_ref, x_ref, g_ref, w_ref, cw_ref, cb_ref, lg_ref, lb_ref,
                         z_ref, u_ref, kwin_ref, vwin_ref, cst_ref,
                         p_ref, y3_ref, k0_ref, k1_ref, v0_ref, v1_ref, q_ref, attn_ref):
    t = x_ref.shape[1]
    j = pl.program_id(1)
    last = pl.num_programs(1) - 1

    @pl.when(j == 0)
    def _():
        p_ref[:, 0:SEG, :] = jnp.zeros((N_SLABS, SEG, LANES), _F32)
        for r in (k0_ref, k1_ref, v0_ref, v1_ref):
            r[0:WINDOW, :] = jnp.zeros((WINDOW, LANES), _BF16)

    h = _rms_rows(x_ref[0], g_ref[...]).astype(_BF16)
    nb = PROJ_BLK
    slabs_per_blk = nb // LANES
    n_blk = D_CONV // nb

    def proj(col0):
        return _dot(h, w_ref[:, col0:col0 + nb])

    def seg_row(i):
        r = i + SEG - CONV_HIST
        return r if r < SEG else r - SEG + SEG_PITCH

    def store_y(c, m, y):
        y3_ref[c, pl.ds(m, SUBLANES, stride=SEG), :] = y

    def conv_block(blk):
        slabs = tuple(range(blk * slabs_per_blk, (blk + 1) * slabs_per_blk))
        _conv_strided(p_ref, seg_row, SEG_PITCH, SEG, cw_ref, cb_ref, store_y, slabs=slabs)

    glu_tail = []
    for blk in range(n_blk):
        glu = proj(C_A + blk * nb) * _sigmoid(proj(C_B + blk * nb))
        glu_tail.append(glu[t - HIST_PAD:, :])
        for c in range(slabs_per_blk):
            for s in range(SUBLANES):
                p_ref[blk * slabs_per_blk + c, (s + 1) * SEG_PITCH:(s + 1) * SEG_PITCH + SEG, :] = (
                    glu[s * SEG:(s + 1) * SEG, c * LANES:(c + 1) * LANES])
        if blk > 0:
            conv_block(blk - 1)
    kv = proj(C_KV)
    q_ref[:, 0:nb] = (proj(C_Q) * SCALE).astype(_BF16)
    conv_block(n_blk - 1)
    p_ref[:, 0:SEG, :] = p_ref[:, SUBLANES * SEG_PITCH:SUBLANES * SEG_PITCH + SEG, :]

    k_new, v_new = kv[:, :KV_W], kv[:, KV_W:]
    kd0, kd1 = _dup_heads(k_new)
    vd0, vd1 = _dup_heads(v_new)
    k0_ref[WINDOW:WINDOW + t, :] = kd0
    k1_ref[WINDOW:WINDOW + t, :] = kd1
    v0_ref[WINDOW:WINDOW + t, :] = vd0
    v1_ref[WINDOW:WINDOW + t, :] = vd1
    for blk in range(1, ATTN_W // nb):
        q_ref[:, blk * nb:(blk + 1) * nb] = (proj(C_Q + blk * nb) * SCALE).astype(_BF16)

    y = _slabs_to_rows(y3_ref)
    mu = jnp.mean(y, axis=-1, keepdims=True)
    yc = y - mu
    rstd = lax.rsqrt(jnp.mean(yc * yc, axis=-1, keepdims=True) + EPS)

    nk = WINDOW + CHUNK
    n_chunks = t // CHUNK
    per = n_chunks // n_blk
    ga = []
    for c in range(n_chunks):
        r0 = c * CHUNK
        kd = (k0_ref[r0:r0 + nk, :], k1_ref[r0:r0 + nk, :])
        vd = (v0_ref[r0:r0 + nk, :], v1_ref[r0:r0 + nk, :])
        key_pos0 = j * t + r0 - WINDOW if r0 < WINDOW else None
        attn_ref[r0:r0 + CHUNK, :] = _attend(q_ref[r0:r0 + CHUNK, :], kd, vd, sink_ref, key_pos0)
        if c % per == per - 1:
            blk = c // per
            cs = slice(blk * nb, (blk + 1) * nb)
            yb = _slabs_to_rows(y3_ref, range(blk * slabs_per_blk, (blk + 1) * slabs_per_blk))
            ln = (yb - mu) * rstd * lg_ref[:, cs] + lb_ref[:, cs]
            z_ref[:, cs] = (_silu(ln) * _silu(proj(C_GC + blk * nb))).astype(_BF16)
            ga.append(proj(C_GA + blk * nb))
    for r in (k0_ref, k1_ref, v0_ref, v1_ref):
        r[0:WINDOW, :] = r[t:t + WINDOW, :]
    for blk in range(ATTN_W // nb):
        cs = slice(blk * nb, (blk + 1) * nb)
        u_ref[:, cs] = (attn_ref[:, cs] * _silu(ga[blk])).astype(_BF16)

    @pl.when(j == last)
    def _():
        cst_ref[0] = jnp.concatenate(glu_tail, axis=1)[HIST_OFF:, :]
        kwin_ref[0] = k_new[t - WINDOW:, :]
        vwin_ref[0] = v_new[t - WINDOW:, :]


def _front_sample_kernel(sink_ref, x_ref, g_ref, w_ref, cw_ref, cb_ref, lg_ref, lb_ref,
                         ck_ref, cv_ref, sc_ref,
                         z_ref, u_ref, kwin_ref, vwin_ref, cst_ref,
                         p_ref, y3_ref, attn_ref):
    n, t = ck_ref.shape[0], x_ref.shape[0] // ck_ref.shape[0]
    h = _rms_rows(x_ref[...], g_ref[...]).astype(_BF16)

    a = _dot(h, w_ref[:, C_A:C_A + D_CONV])
    b = _dot(h, w_ref[:, C_B:C_B + D_CONV])
    glu = a * _sigmoid(b)
    for i in range(n):
        base = i * STREAM_PITCH
        for c in range(N_SLABS):
            cs = slice(c * LANES, (c + 1) * LANES)
            p_ref[c, base + HIST_OFF:base + HIST_PAD, :] = sc_ref[i, :, cs]
            p_ref[c, base + HIST_PAD:base + HIST_PAD + t, :] = glu[i * t:(i + 1) * t, cs]
            cst_ref[i, :, cs] = p_ref[c, base + HIST_PAD + t - CONV_HIST:base + HIST_PAD + t, :]
    for grp in range(n // SUBLANES):

        def store_y(c, m, y, grp=grp):
            y3_ref[c, pl.ds(grp * SUBLANES * t + m, SUBLANES, stride=t), :] = y

        _conv_strided(p_ref, lambda i, grp=grp: grp * SUBLANES * STREAM_PITCH + HIST_OFF + i, STREAM_PITCH, t,
                      cw_ref, cb_ref, store_y)
    gc = _dot(h, w_ref[:, C_GC:C_GC + D_CONV])
    z_ref[...] = _conv_gate(_slabs_to_rows(y3_ref), gc, lg_ref[...], lb_ref[...]).astype(_BF16)

    q = (_dot(h, w_ref[:, C_Q:C_Q + ATTN_W]) * SCALE).astype(_BF16)
    kv = _dot(h, w_ref[:, C_KV:C_KV + 2 * KV_W])
    for i in range(n):
        k_all = jnp.concatenate([ck_ref[i], kv[i * t:(i + 1) * t, :KV_W]], axis=0)
        v_all = jnp.concatenate([cv_ref[i], kv[i * t:(i + 1) * t, KV_W:]], axis=0)
        kwin_ref[i] = k_all[t:, :]
        vwin_ref[i] = v_all[t:, :]
        attn_ref[i * t:(i + 1) * t, :] = _attend(q[i * t:(i + 1) * t, :], _dup_heads(k_all), _dup_heads(v_all),
                                                 sink_ref, None)
    ga = _dot(h, w_ref[:, C_GA:C_GA + ATTN_W])
    u_ref[...] = (attn_ref[...] * _silu(ga)).astype(_BF16)


def _back_kernel(x_ref, z_ref, u_ref, g_ref, wm_ref, wpw_ref, wo_ref, wout_ref, fg_ref, y_ref, m_ref):
    x = x_ref[...]
    h = _rms_rows(x, g_ref[...]).astype(_BF16)
    z = z_ref[...]
    u = u_ref[...]
    nb = 512
    for c in range(D_MODEL // nb):
        cs = slice(c * nb, (c + 1) * nb)
        mc = _dot(h, wm_ref[:, c * nb:(c + 1) * nb])
        ma = _dot(h, wm_ref[:, D_MODEL + c * nb:D_MODEL + (c + 1) * nb])
        conv_o = _dot(z, wpw_ref[:, cs])
        attn_o = _dot(u, wo_ref[:, cs])
        m_ref[:, cs] = (_sigmoid(mc) * conv_o + _sigmoid(ma) * attn_o).astype(_BF16)
    y_ref[...] = _rms_rows(x + _dot(m_ref[...], wout_ref[...]), fg_ref[...])


def _cast_kernel(w_ref, wf_ref, wm_ref):
    wf_ref[...] = w_ref[0, :, :FRONT_COLS].astype(_BF16)
    wm_ref[...] = w_ref[0, :, FRONT_COLS:].astype(_BF16)


_VMEM = pl.BlockSpec(memory_space=pltpu.VMEM)
_SMEM = pl.BlockSpec(memory_space=pltpu.SMEM)


def _cast_w_in(w_in):
    rows, cols = w_in.shape[1:]
    return pl.pallas_call(
        _cast_kernel,
        grid=(rows // CAST_ROWS,),
        in_specs=[pl.BlockSpec((1, CAST_ROWS, cols), lambda i: (0, i, 0))],
        out_specs=[pl.BlockSpec((CAST_ROWS, FRONT_COLS), lambda i: (i, 0)),
                   pl.BlockSpec((CAST_ROWS, cols - FRONT_COLS), lambda i: (i, 0))],
        out_shape=[jax.ShapeDtypeStruct((rows, FRONT_COLS), _BF16),
                   jax.ShapeDtypeStruct((rows, cols - FRONT_COLS), _BF16)],
        compiler_params=pltpu.CompilerParams(dimension_semantics=("arbitrary",), vmem_limit_bytes=VMEM_LIMIT),
        name="cast_w_in",
    )(w_in)


def _front_prompt(x, sink, norm_g, w_front, cw, cb, lg, lb):
    bsz, seq, _ = x.shape
    t = PROMPT_TILE
    nt = seq // t
    tok = lambda width, dt: jax.ShapeDtypeStruct((bsz * seq, width), dt)
    return pl.pallas_call(
        _front_prompt_kernel,
        grid=(bsz, nt),
        in_specs=[_SMEM, pl.BlockSpec((1, t, D_MODEL), lambda b, j: (b, j, 0)),
                  _VMEM, _VMEM, _VMEM, _VMEM, _VMEM, _VMEM],
        out_specs=[pl.BlockSpec((t, D_CONV), lambda b, j: (b * nt + j, 0)),
                   pl.BlockSpec((t, ATTN_W), lambda b, j: (b * nt + j, 0)),
                   pl.BlockSpec((1, WINDOW, KV_W), lambda b, j: (b, 0, 0)),
                   pl.BlockSpec((1, WINDOW, KV_W), lambda b, j: (b, 0, 0)),
                   pl.BlockSpec((1, CONV_HIST, D_CONV), lambda b, j: (b, 0, 0))],
        out_shape=[tok(D_CONV, _BF16), tok(ATTN_W, _BF16),
                   jax.ShapeDtypeStruct((bsz, WINDOW, KV_W), _F32),
                   jax.ShapeDtypeStruct((bsz, WINDOW, KV_W), _F32),
                   jax.ShapeDtypeStruct((bsz, CONV_HIST, D_CONV), _F32)],
        scratch_shapes=[pltpu.VMEM((N_SLABS, (SUBLANES + 1) * SEG_PITCH, LANES), _F32),
                        pltpu.VMEM((N_SLABS, t, LANES), _F32),
                        pltpu.VMEM((WINDOW + t, LANES), _BF16),
                        pltpu.VMEM((WINDOW + t, LANES), _BF16),
                        pltpu.VMEM((WINDOW + t, LANES), _BF16),
                        pltpu.VMEM((WINDOW + t, LANES), _BF16),
                        pltpu.VMEM((t, ATTN_W), _BF16),
                        pltpu.VMEM((t, ATTN_W), _F32)],
        compiler_params=pltpu.CompilerParams(dimension_semantics=("arbitrary", "arbitrary"),
                                             vmem_limit_bytes=VMEM_LIMIT),
        name="front_prompt",
    )(sink, x, norm_g, w_front, cw, cb, lg, lb)


def _front_sample(x2d, sink, norm_g, w_front, cw, cb, lg, lb, cache_k, cache_v, state_conv):
    n = cache_k.shape[0]
    ntok = x2d.shape[0]
    t = ntok // n
    return pl.pallas_call(
        _front_sample_kernel,
        in_specs=[_SMEM] + [_VMEM] * 10,
        out_specs=[_VMEM] * 5,
        out_shape=[jax.ShapeDtypeStruct((ntok, D_CONV), _BF16),
                   jax.ShapeDtypeStruct((ntok, ATTN_W), _BF16),
                   jax.ShapeDtypeStruct((n, WINDOW, KV_W), _F32),
                   jax.ShapeDtypeStruct((n, WINDOW, KV_W), _F32),
                   jax.ShapeDtypeStruct((n, CONV_HIST, D_CONV), _F32)],
        scratch_shapes=[pltpu.VMEM((N_SLABS, n * STREAM_PITCH, LANES), _F32),
                        pltpu.VMEM((N_SLABS, ntok, LANES), _F32),
                        pltpu.VMEM((ntok, ATTN_W), _F32)],
        compiler_params=pltpu.CompilerParams(vmem_limit_bytes=VMEM_LIMIT),
        name="front_sample",
    )(sink, x2d, norm_g, w_front, cw, cb, lg, lb, cache_k, cache_v, state_conv)


def _back(x2d, z, u, norm_g, w_merge, w_pw, w_o, w_out, final_g, t):
    ntok = x2d.shape[0]
    row = lambda width: pl.BlockSpec((t, width), lambda i: (i, 0))
    return pl.pallas_call(
        _back_kernel,
        grid=(ntok // t,),
        in_specs=[row(D_MODEL), row(D_CONV), row(ATTN_W), _VMEM, _VMEM, _VMEM, _VMEM, _VMEM, _VMEM],
        out_specs=row(D_MODEL),
        out_shape=jax.ShapeDtypeStruct((ntok, D_MODEL), _F32),
        scratch_shapes=[pltpu.VMEM((t, D_MODEL), _BF16)],
        compiler_params=pltpu.CompilerParams(dimension_semantics=("arbitrary",),
                                             vmem_limit_bytes=VMEM_LIMIT),
        name="back",
    )(x2d, z, u, norm_g, w_merge, w_pw, w_o, w_out, final_g)


def kernel(x_prompt, x_sample, cache_k, cache_v, state_conv, norm_g, w_in, conv_w, conv_b, ln_g, ln_b,
           w_conv_pw, attn_sink, w_o_attn, w_out, final_g):
    depth = w_in.shape[0]
    assert depth == 1, "single-layer step only"
    bsz, seq, _ = x_prompt.shape
    n, t_s, _ = x_sample.shape
    assert seq % PROMPT_TILE == 0 and PROMPT_TILE % CHUNK == 0 and PROMPT_TILE >= WINDOW
    assert SEG >= HIST_PAD and SEG % CONV_GROUP == 0 and (bsz * seq) % BACK_TILE == 0
    assert (PROMPT_TILE // CHUNK) % (D_CONV // PROJ_BLK) == 0 and D_CONV == ATTN_W
    assert w_in.shape[1] % CAST_ROWS == 0 and w_in.shape[2] == FRONT_COLS + MERGE_COLS
    assert n % SUBLANES == 0 and t_s % CONV_GROUP == 0 and HIST_PAD + t_s <= STREAM_PITCH

    g = norm_g[0].reshape(1, D_MODEL)
    fg = final_g.reshape(1, D_MODEL)
    w_front, w_merge = _cast_w_in(w_in)
    w_pw = w_conv_pw[0].astype(_BF16)
    w_o = w_o_attn[0].astype(_BF16)
    w_ob = w_out[0].astype(_BF16)
    cw = conv_w[0]
    cb = conv_b[0].reshape(1, D_CONV)
    lg = ln_g[0].reshape(1, D_CONV)
    lb = ln_b[0].reshape(1, D_CONV)
    sink = attn_sink[0]

    xp2d = x_prompt.reshape(bsz * seq, D_MODEL)
    z_p, u_p, kw_p, vw_p, cs_p = _front_prompt(x_prompt, sink, g, w_front, cw, cb, lg, lb)
    y_p = _back(xp2d, z_p, u_p, g, w_merge, w_pw, w_o, w_ob, fg, BACK_TILE)

    xs2d = x_sample.reshape(n * t_s, D_MODEL)
    ck = cache_k[0].reshape(n, WINDOW, KV_W)
    cv = cache_v[0].reshape(n, WINDOW, KV_W)
    z_s, u_s, kw_s, vw_s, cs_s = _front_sample(xs2d, sink, g, w_front, cw, cb, lg, lb, ck, cv, state_conv[0])
    y_s = _back(xs2d, z_s, u_s, g, w_merge, w_pw, w_o, w_ob, fg, n * t_s)

    kv_shape = lambda rows: (depth, rows, WINDOW, N_KV_HEADS, HEAD_DIM)
    return (y_p.reshape(bsz, seq, D_MODEL), y_s.reshape(n, t_s, D_MODEL),
            kw_p.reshape(kv_shape(bsz)), vw_p.reshape(kv_shape(bsz)), cs_p.reshape(depth, bsz, CONV_HIST, D_CONV),
            kw_s.reshape(kv_shape(n)), vw_s.reshape(kv_shape(n)), cs_s.reshape(depth, n, CONV_HIST, D_CONV))
```

```python
import jax
import jax.numpy as jnp
from jax import lax
from jax.experimental import pallas as pl
from jax.experimental.pallas import tpu as pltpu

D_MODEL = 2048
D_CONV = 1024
CONV_WIDTH = 31
CONV_HIST = CONV_WIDTH - 1
N_HEADS = 16
N_KV_HEADS = 2
HEAD_DIM = 64
GROUP = N_HEADS // N_KV_HEADS
ATTN_W = N_HEADS * HEAD_DIM
KV_W = N_KV_HEADS * HEAD_DIM
CHUNK = 64
WINDOW = 128
EPS = 1e-6
NEG = -1e30
SCALE = HEAD_DIM ** -0.5

C_A, C_B, C_GC = 0, D_CONV, 2 * D_CONV
C_Q = 3 * D_CONV
C_KV = C_Q + ATTN_W
C_GA = C_KV + 2 * KV_W
FRONT_COLS = C_GA + ATTN_W
MERGE_COLS = 2 * D_MODEL

LANES = 128
SUBLANES = 8
HIST_PAD = 32
HIST_OFF = HIST_PAD - CONV_HIST
PROMPT_TILE = 512
BACK_TILE = 512
BACK_ROWS = 256
BACK_BLK = 512
CAST_ROWS = 256
SEG = PROMPT_TILE // SUBLANES
SEG_PITCH = SEG + 4
STREAM_PITCH = 52
CONV_GROUP = 4
N_SLABS = D_CONV // LANES
PROJ_BLK = 2 * LANES
VMEM_LIMIT = 58 * 1024 * 1024

_SLOPES = [2.0 ** (-8.0 * (h + 1) / N_HEADS) for h in range(N_HEADS)]

_F32 = jnp.float32
_BF16 = jnp.bfloat16


def _sigmoid(x):
    return 1.0 / (1.0 + jnp.exp(-x))


def _silu(x):
    return x * _sigmoid(x)


def _rms_rows(x, g):
    return x * lax.rsqrt(jnp.mean(x * x, axis=-1, keepdims=True) + EPS) * g


def _dot(a, b):
    return jnp.dot(a, b, preferred_element_type=_F32)


def _conv_strided(p_ref, row_of, stride, n_out, cw_ref, cb_ref, store, slabs=None):
    for c in (range(N_SLABS) if slabs is None else slabs):
        cs = slice(c * LANES, (c + 1) * LANES)
        bias = jnp.broadcast_to(cb_ref[:, cs], (SUBLANES, LANES))
        for m0 in range(0, n_out, CONV_GROUP):
            accs = [bias] * CONV_GROUP
            loaded = {}
            for j in range(CONV_WIDTH):
                w = cw_ref[j:j + 1, cs]
                for g in range(CONV_GROUP):
                    i = m0 + g + j
                    if i not in loaded:
                        loaded[i] = p_ref[c, pl.ds(row_of(i), SUBLANES, stride=stride), :]
                    accs[g] = accs[g] + loaded[i] * w
            for g in range(CONV_GROUP):
                store(c, m0 + g, accs[g])


def _slabs_to_rows(y3_ref, slabs=None):
    return jnp.concatenate([y3_ref[c] for c in (range(N_SLABS) if slabs is None else slabs)], axis=1)


def _conv_gate(y, gc, lg, lb):
    mu = jnp.mean(y, axis=-1, keepdims=True)
    yc = y - mu
    ln = yc * lax.rsqrt(jnp.mean(yc * yc, axis=-1, keepdims=True) + EPS) * lg + lb
    return _silu(ln) * _silu(gc)


def _attend(q, kd, vd, sink_ref, key_pos0):
    tq = q.shape[0]
    nk = kd[0].shape[0]
    lane = lax.broadcasted_iota(jnp.int32, (tq, LANES), 1)
    low_half = lane < HEAD_DIM
    qi = lax.broadcasted_iota(jnp.int32, (tq, nk), 0)
    sj = lax.broadcasted_iota(jnp.int32, (tq, nk), 1)
    dist = jnp.abs(qi + WINDOW - sj).astype(_F32)
    valid = None if key_pos0 is None else (sj + key_pos0) >= 0
    zero = jnp.zeros((), _BF16)
    outs = []
    for kv in range(N_KV_HEADS):
        blocks = []
        for g in range(GROUP):
            h = kv * GROUP + g
            qb = q[:, (h // 2) * LANES:(h // 2 + 1) * LANES]
            blocks.append(jnp.where(low_half if h % 2 == 0 else jnp.logical_not(low_half), qb, zero))
        qs = jnp.concatenate(blocks, axis=0)
        s_all = lax.dot_general(qs, kd[kv], (((1,), (1,)), ((), ())), preferred_element_type=_F32)
        probs, inv = [], []
        for g in range(GROUP):
            h = kv * GROUP + g
            s = s_all[g * tq:(g + 1) * tq] - _SLOPES[h] * dist
            if valid is not None:
                s = jnp.where(valid, s, NEG)
            sink = sink_ref[h]
            m = jnp.maximum(jnp.max(s, axis=-1, keepdims=True), sink)
            e = jnp.exp(s - m)
            denom = jnp.sum(e, axis=-1, keepdims=True) + jnp.exp(sink - m)
            probs.append(e.astype(_BF16))
            inv.append(1.0 / denom)
        p_all = jnp.concatenate(probs, axis=0)
        o_all = _dot(p_all, vd[kv]) * jnp.concatenate(inv, axis=0)
        for pair in range(GROUP // 2):
            outs.append(jnp.where(low_half, o_all[(2 * pair) * tq:(2 * pair + 1) * tq],
                                  o_all[(2 * pair + 1) * tq:(2 * pair + 2) * tq]))
    return jnp.concatenate(outs, axis=1)


def _dup_heads(kv_rows):
    kvb = kv_rows.astype(_BF16)
    h0, h1 = kvb[:, :HEAD_DIM], kvb[:, HEAD_DIM:]
    return jnp.concatenate([h0, h0], axis=1), jnp.concatenate([h1, h1], axis=1)


def _front_prompt_kernel(sink_ref, x_ref, g_ref, w_ref, cw_ref, cb_ref, lg_ref, lb_ref,
                         z_ref, u_ref, kwin_ref, vwin_ref, cst_ref,
                         p_ref, y3_ref, k0_ref, k1_ref, v0_ref, v1_ref, q_ref, attn_ref):
    t = x_ref.shape[1]
    j = pl.program_id(1)
    last = pl.num_programs(1) - 1

    @pl.when(j == 0)
    def _():
        p_ref[:, 0:SEG, :] = jnp.zeros((N_SLABS, SEG, LANES), _F32)
        for r in (k0_ref, k1_ref, v0_ref, v1_ref):
            r[0:WINDOW, :] = jnp.zeros((WINDOW, LANES), _BF16)

    h = _rms_rows(x_ref[0], g_ref[...]).astype(_BF16)
    nb = PROJ_BLK
    slabs_per_blk = nb // LANES
    n_blk = D_CONV // nb

    def proj(col0):
        return _dot(h, w_ref[:, col0:col0 + nb])

    def seg_row(i):
        r = i + SEG - CONV_HIST
        return r if r < SEG else r - SEG + SEG_PITCH

    def store_y(c, m, y):
        y3_ref[c, pl.ds(m, SUBLANES, stride=SEG), :] = y

    def conv_block(blk):
        slabs = tuple(range(blk * slabs_per_blk, (blk + 1) * slabs_per_blk))
        _conv_strided(p_ref, seg_row, SEG_PITCH, SEG, cw_ref, cb_ref, store_y, slabs=slabs)

    glu_tail = []
    for blk in range(n_blk):
        glu = proj(C_A + blk * nb) * _sigmoid(proj(C_B + blk * nb))
        glu_tail.append(glu[t - HIST_PAD:, :])
        for c in range(slabs_per_blk):
            for s in range(SUBLANES):
                p_ref[blk * slabs_per_blk + c, (s + 1) * SEG_PITCH:(s + 1) * SEG_PITCH + SEG, :] = (
                    glu[s * SEG:(s + 1) * SEG, c * LANES:(c + 1) * LANES])
        if blk > 0:
            conv_block(blk - 1)
    kv = proj(C_KV)
    q_ref[:, 0:nb] = (proj(C_Q) * SCALE).astype(_BF16)
    conv_block(n_blk - 1)
    p_ref[:, 0:SEG, :] = p_ref[:, SUBLANES * SEG_PITCH:SUBLANES * SEG_PITCH + SEG, :]

    k_new, v_new = kv[:, :KV_W], kv[:, KV_W:]
    kd0, kd1 = _dup_heads(k_new)
    vd0, vd1 = _dup_heads(v_new)
    k0_ref[WINDOW:WINDOW + t, :] = kd0
    k1_ref[WINDOW:WINDOW + t, :] = kd1
    v0_ref[WINDOW:WINDOW + t, :] = vd0
    v1_ref[WINDOW:WINDOW + t, :] = vd1
    for blk in range(1, ATTN_W // nb):
        q_ref[:, blk * nb:(blk + 1) * nb] = (proj(C_Q + blk * nb) * SCALE).astype(_BF16)

    y = _slabs_to_rows(y3_ref)
    mu = jnp.mean(y, axis=-1, keepdims=True)
    yc = y - mu
    rstd = lax.rsqrt(jnp.mean(yc * yc, axis=-1, keepdims=True) + EPS)

    nk = WINDOW + CHUNK
    n_chunks = t // CHUNK
    per = n_chunks // n_blk
    ga = []
    for c in range(n_chunks):
        r0 = c * CHUNK
        kd = (k0_ref[r0:r0 + nk, :], k1_ref[r0:r0 + nk, :])
        vd = (v0_ref[r0:r0 + nk, :], v1_ref[r0:r0 + nk, :])
        key_pos0 = j * t + r0 - WINDOW if r0 < WINDOW else None
        attn_ref[r0:r0 + CHUNK, :] = _attend(q_ref[r0:r0 + CHUNK, :], kd, vd, sink_ref, key_pos0)
        if c % per == per - 1:
            blk = c // per
            cs = slice(blk * nb, (blk + 1) * nb)
            yb = _slabs_to_rows(y3_ref, range(blk * slabs_per_blk, (blk + 1) * slabs_per_blk))
            ln = (yb - mu) * rstd * lg_ref[:, cs] + lb_ref[:, cs]
            z_ref[:, cs] = (_silu(ln) * _silu(proj(C_GC + blk * nb))).astype(_BF16)
            ga.append(proj(C_GA + blk * nb))
    for r in (k0_ref, k1_ref, v0_ref, v1_ref):
        r[0:WINDOW, :] = r[t:t + WINDOW, :]
    for blk in range(ATTN_W // nb):
        cs = slice(blk * nb, (blk + 1) * nb)
        u_ref[:, cs] = (attn_ref[:, cs] * _silu(ga[blk])).astype(_BF16)

    @pl.when(j == last)
    def _():
        cst_ref[0] = jnp.concatenate(glu_tail, axis=1)[HIST_OFF:, :]
        kwin_ref[0] = k_new[t - WINDOW:, :]
        vwin_ref[0] = v_new[t - WINDOW:, :]


def _front_sample_kernel(sink_ref, x_ref, g_ref, w_ref, cw_ref, cb_ref, lg_ref, lb_ref,
                         ck_ref, cv_ref, sc_ref,
                         z_ref, u_ref, kwin_ref, vwin_ref, cst_ref,
                         p_ref, y3_ref, attn_ref):
    n, t = ck_ref.shape[0], x_ref.shape[0] // ck_ref.shape[0]
    h = _rms_rows(x_ref[...], g_ref[...]).astype(_BF16)

    a = _dot(h, w_ref[:, C_A:C_A + D_CONV])
    b = _dot(h, w_ref[:, C_B:C_B + D_CONV])
    glu = a * _sigmoid(b)
    for i in range(n):
        base = i * STREAM_PITCH
        for c in range(N_SLABS):
            cs = slice(c * LANES, (c + 1) * LANES)
            p_ref[c, base + HIST_OFF:base + HIST_PAD, :] = sc_ref[i, :, cs]
            p_ref[c, base + HIST_PAD:base + HIST_PAD + t, :] = glu[i * t:(i + 1) * t, cs]
            cst_ref[i, :, cs] = p_ref[c, base + HIST_PAD + t - CONV_HIST:base + HIST_PAD + t, :]
    for grp in range(n // SUBLANES):

        def store_y(c, m, y, grp=grp):
            y3_ref[c, pl.ds(grp * SUBLANES * t + m, SUBLANES, stride=t), :] = y

        _conv_strided(p_ref, lambda i, grp=grp: grp * SUBLANES * STREAM_PITCH + HIST_OFF + i, STREAM_PITCH, t,
                      cw_ref, cb_ref, store_y)
    gc = _dot(h, w_ref[:, C_GC:C_GC + D_CONV])
    z_ref[...] = _conv_gate(_slabs_to_rows(y3_ref), gc, lg_ref[...], lb_ref[...]).astype(_BF16)

    q = (_dot(h, w_ref[:, C_Q:C_Q + ATTN_W]) * SCALE).astype(_BF16)
    kv = _dot(h, w_ref[:, C_KV:C_KV + 2 * KV_W])
    for i in range(n):
        k_all = jnp.concatenate([ck_ref[i], kv[i * t:(i + 1) * t, :KV_W]], axis=0)
        v_all = jnp.concatenate([cv_ref[i], kv[i * t:(i + 1) * t, KV_W:]], axis=0)
        kwin_ref[i] = k_all[t:, :]
        vwin_ref[i] = v_all[t:, :]
        attn_ref[i * t:(i + 1) * t, :] = _attend(q[i * t:(i + 1) * t, :], _dup_heads(k_all), _dup_heads(v_all),
                                                 sink_ref, None)
    ga = _dot(h, w_ref[:, C_GA:C_GA + ATTN_W])
    u_ref[...] = (attn_ref[...] * _silu(ga)).astype(_BF16)


def _back_kernel(x_ref, z_ref, u_ref, g_ref, wm_ref, wpw_ref, wo_ref, wout_ref, fg_ref, y_ref, m_ref):
    nb = BACK_BLK
    t = x_ref.shape[0]
    half = min(t, BACK_ROWS)
    for r0 in range(0, t, half):
        rows = slice(r0, r0 + half)
        x = x_ref[rows, :]
        h = _rms_rows(x, g_ref[...]).astype(_BF16)
        z = z_ref[rows, :]
        u = u_ref[rows, :]
        for c in range(D_MODEL // nb):
            cs = slice(c * nb, (c + 1) * nb)
            mc = _dot(h, wm_ref[:, c * nb:(c + 1) * nb])
            ma = _dot(h, wm_ref[:, D_MODEL + c * nb:D_MODEL + (c + 1) * nb])
            conv_o = _dot(z, wpw_ref[:, cs])
            attn_o = _dot(u, wo_ref[:, cs])
            m_ref[rows, cs] = (_sigmoid(mc) * conv_o + _sigmoid(ma) * attn_o).astype(_BF16)
        y_ref[rows, :] = _rms_rows(x + _dot(m_ref[rows, :], wout_ref[...]), fg_ref[...])


def _cast_kernel(w_ref, wf_ref, wm_ref):
    wf_ref[...] = w_ref[0, :, :FRONT_COLS].astype(_BF16)
    wm_ref[...] = w_ref[0, :, FRONT_COLS:].astype(_BF16)


_VMEM = pl.BlockSpec(memory_space=pltpu.VMEM)
_SMEM = pl.BlockSpec(memory_space=pltpu.SMEM)


def _cast_w_in(w_in):
    rows, cols = w_in.shape[1:]
    return pl.pallas_call(
        _cast_kernel,
        grid=(rows // CAST_ROWS,),
        in_specs=[pl.BlockSpec((1, CAST_ROWS, cols), lambda i: (0, i, 0))],
        out_specs=[pl.BlockSpec((CAST_ROWS, FRONT_COLS), lambda i: (i, 0)),
                   pl.BlockSpec((CAST_ROWS, cols - FRONT_COLS), lambda i: (i, 0))],
        out_shape=[jax.ShapeDtypeStruct((rows, FRONT_COLS), _BF16),
                   jax.ShapeDtypeStruct((rows, cols - FRONT_COLS), _BF16)],
        compiler_params=pltpu.CompilerParams(dimension_semantics=("arbitrary",), vmem_limit_bytes=VMEM_LIMIT),
        name="cast_w_in",
    )(w_in)


def _front_prompt(x, sink, norm_g, w_front, cw, cb, lg, lb):
    bsz, seq, _ = x.shape
    t = PROMPT_TILE
    nt = seq // t
    tok = lambda width, dt: jax.ShapeDtypeStruct((bsz * seq, width), dt)
    return pl.pallas_call(
        _front_prompt_kernel,
        grid=(bsz, nt),
        in_specs=[_SMEM, pl.BlockSpec((1, t, D_MODEL), lambda b, j: (b, j, 0)),
                  _VMEM, _VMEM, _VMEM, _VMEM, _VMEM, _VMEM],
        out_specs=[pl.BlockSpec((t, D_CONV), lambda b, j: (b * nt + j, 0)),
                   pl.BlockSpec((t, ATTN_W), lambda b, j: (b * nt + j, 0)),
                   pl.BlockSpec((1, WINDOW, KV_W), lambda b, j: (b, 0, 0)),
                   pl.BlockSpec((1, WINDOW, KV_W), lambda b, j: (b, 0, 0)),
                   pl.BlockSpec((1, CONV_HIST, D_CONV), lambda b, j: (b, 0, 0))],
        out_shape=[tok(D_CONV, _BF16), tok(ATTN_W, _BF16),
                   jax.ShapeDtypeStruct((bsz, WINDOW, KV_W), _F32),
                   jax.ShapeDtypeStruct((bsz, WINDOW, KV_W), _F32),
                   jax.ShapeDtypeStruct((bsz, CONV_HIST, D_CONV), _F32)],
        scratch_shapes=[pltpu.VMEM((N_SLABS, (SUBLANES + 1) * SEG_PITCH, LANES), _F32),
                        pltpu.VMEM((N_SLABS, t, LANES), _F32),
                        pltpu.VMEM((WINDOW + t, LANES), _BF16),
                        pltpu.VMEM((WINDOW + t, LANES), _BF16),
                        pltpu.VMEM((WINDOW + t, LANES), _BF16),
                        pltpu.VMEM((WINDOW + t, LANES), _BF16),
                        pltpu.VMEM((t, ATTN_W), _BF16),
                        pltpu.VMEM((t, ATTN_W), _F32)],
        compiler_params=pltpu.CompilerParams(dimension_semantics=("arbitrary", "arbitrary"),
                                             vmem_limit_bytes=VMEM_LIMIT),
        name="front_prompt",
    )(sink, x, norm_g, w_front, cw, cb, lg, lb)


def _front_sample(x2d, sink, norm_g, w_front, cw, cb, lg, lb, cache_k, cache_v, state_conv):
    n = cache_k.shape[0]
    ntok = x2d.shape[0]
    t = ntok // n
    return pl.pallas_call(
        _front_sample_kernel,
        in_specs=[_SMEM] + [_VMEM] * 10,
        out_specs=[_VMEM] * 5,
        out_shape=[jax.ShapeDtypeStruct((ntok, D_CONV), _BF16),
                   jax.ShapeDtypeStruct((ntok, ATTN_W), _BF16),
                   jax.ShapeDtypeStruct((n, WINDOW, KV_W), _F32),
                   jax.ShapeDtypeStruct((n, WINDOW, KV_W), _F32),
                   jax.ShapeDtypeStruct((n, CONV_HIST, D_CONV), _F32)],
        scratch_shapes=[pltpu.VMEM((N_SLABS, n * STREAM_PITCH, LANES), _F32),
                        pltpu.VMEM((N_SLABS, ntok, LANES), _F32),
                        pltpu.VMEM((ntok, ATTN_W), _F32)],
        compiler_params=pltpu.CompilerParams(vmem_limit_bytes=VMEM_LIMIT),
        name="front_sample",
    )(sink, x2d, norm_g, w_front, cw, cb, lg, lb, cache_k, cache_v, state_conv)


def _back(x2d, z, u, norm_g, w_merge, w_pw, w_o, w_out, final_g, t):
    ntok = x2d.shape[0]
    row = lambda width: pl.BlockSpec((t, width), lambda i: (i, 0))
    return pl.pallas_call(
        _back_kernel,
        grid=(ntok // t,),
        in_specs=[row(D_MODEL), row(D_CONV), row(ATTN_W), _VMEM, _VMEM, _VMEM, _VMEM, _VMEM, _VMEM],
        out_specs=row(D_MODEL),
        out_shape=jax.ShapeDtypeStruct((ntok, D_MODEL), _F32),
        scratch_shapes=[pltpu.VMEM((t, D_MODEL), _BF16)],
        compiler_params=pltpu.CompilerParams(dimension_semantics=("arbitrary",),
                                             vmem_limit_bytes=VMEM_LIMIT),
        name="back",
    )(x2d, z, u, norm_g, w_merge, w_pw, w_o, w_out, final_g)


def kernel(x_prompt, x_sample, cache_k, cache_v, state_conv, norm_g, w_in, conv_w, conv_b, ln_g, ln_b,
           w_conv_pw, attn_sink, w_o_attn, w_out, final_g):
    depth = w_in.shape[0]
    assert depth == 1, "single-layer step only"
    bsz, seq, _ = x_prompt.shape
    n, t_s, _ = x_sample.shape
    assert seq % PROMPT_TILE == 0 and PROMPT_TILE % CHUNK == 0 and PROMPT_TILE >= WINDOW
    assert SEG >= HIST_PAD and SEG % CONV_GROUP == 0 and (bsz * seq) % BACK_TILE == 0
    assert (PROMPT_TILE // CHUNK) % (D_CONV // PROJ_BLK) == 0 and D_CONV == ATTN_W
    assert w_in.shape[1] % CAST_ROWS == 0 and w_in.shape[2] == FRONT_COLS + MERGE_COLS
    assert n % SUBLANES == 0 and t_s % CONV_GROUP == 0 and HIST_PAD + t_s <= STREAM_PITCH

    g = norm_g[0].reshape(1, D_MODEL)
    fg = final_g.reshape(1, D_MODEL)
    w_front, w_merge = _cast_w_in(w_in)
    w_pw = w_conv_pw[0].astype(_BF16)
    w_o = w_o_attn[0].astype(_BF16)
    w_ob = w_out[0].astype(_BF16)
    cw = conv_w[0]
    cb = conv_b[0].reshape(1, D_CONV)
    lg = ln_g[0].reshape(1, D_CONV)
    lb = ln_b[0].reshape(1, D_CONV)
    sink = attn_sink[0]

    xp2d = x_prompt.reshape(bsz * seq, D_MODEL)
    z_p, u_p, kw_p, vw_p, cs_p = _front_prompt(x_prompt, sink, g, w_front, cw, cb, lg, lb)
    y_p = _back(xp2d, z_p, u_p, g, w_merge, w_pw, w_o, w_ob, fg, BACK_TILE)

    xs2d = x_sample.reshape(n * t_s, D_MODEL)
    ck = cache_k[0].reshape(n, WINDOW, KV_W)
    cv = cache_v[0].reshape(n, WINDOW, KV_W)
    z_s, u_s, kw_s, vw_s, cs_s = _front_sample(xs2d, sink, g, w_front, cw, cb, lg, lb, ck, cv, state_conv[0])
    y_s = _back(xs2d, z_s, u_s, g, w_merge, w_pw, w_o, w_ob, fg, n * t_s)

    kv_shape = lambda rows: (depth, rows, WINDOW, N_KV_HEADS, HEAD_DIM)
    return (y_p.reshape(bsz, seq, D_MODEL), y_s.reshape(n, t_s, D_MODEL),
            kw_p.reshape(kv_shape(bsz)), vw_p.reshape(kv_shape(bsz)), cs_p.reshape(depth, bsz, CONV_HIST, D_CONV),
            kw_s.reshape(kv_shape(n)), vw_s.reshape(kv_shape(n)), cs_s.reshape(depth, n, CONV_HIST, D_CONV))
```
